```python
import jax, jax.numpy as jnp
from jax import lax
import numpy as np

D_MODEL = 2048
BATCH = 1
SEQ = 16384
DEPTH = 1
DEC_BATCH = 2
DEC_SEQ = 8192
PAST_LEN = 128

GRID_W = 64
N_HEADS = 8
N_KV_HEADS = 2
HEAD_DIM = 128
ATTN_WIDTH = N_HEADS * HEAD_DIM
KV_WIDTH = N_KV_HEADS * HEAD_DIM
ROPE_THETA = 10000.0
Q_BLOCK = 128
LRU_WIDTH = D_MODEL - ATTN_WIDTH
LRU_BLOCKS = 16
LRU_BLOCK_W = LRU_WIDTH // LRU_BLOCKS
CONV_W = 4
CONV_PAD_LEFT = 2
LRU_C = 8.0
MIX_WIDTH = ATTN_WIDTH + LRU_WIDTH
IN_WIDTH = ATTN_WIDTH + 2 * KV_WIDTH + 2 * LRU_WIDTH
N_KEYS = 128
N_EXPERTS = N_KEYS * N_KEYS
PEER_HEADS = 8
PEER_KEY_DIM = 128
PEER_HALF = PEER_KEY_DIM // 2
PEER_TOPK = 16
TOKEN_BLOCK = 128
EPS = 1e-6

kernel_name = 'hymba_hawk_axialgqa_peer_encoder'


def rmsnorm(x, g):
    xf = x.astype(jnp.float32)
    y = xf * lax.rsqrt(jnp.mean(xf * xf, axis=-1, keepdims=True) + EPS)
    return (y * g.astype(jnp.float32)).astype(x.dtype)


def axial_angles(seq):
    n_rows = seq // GRID_W
    row = jnp.repeat(jnp.arange(n_rows, dtype=jnp.float32), GRID_W)
    col = jnp.tile(jnp.arange(GRID_W, dtype=jnp.float32), n_rows)
    half = HEAD_DIM // 2
    inv = ROPE_THETA ** (-jnp.arange(0, half, 2, dtype=jnp.float32) / half)
    return row[:, None] * inv, col[:, None] * inv


def rope_1d(x, ang):
    m = ang.shape[-1]
    c = jnp.cos(ang)[:, None, :]
    s = jnp.sin(ang)[:, None, :]
    x1, x2 = x[..., :m], x[..., m:]
    return jnp.concatenate([x1 * c - x2 * s, x2 * c + x1 * s], axis=-1)


def axial_rope(x, ang_row, ang_col):
    half = HEAD_DIM // 2
    xf = x.astype(jnp.float32)
    y = jnp.concatenate([rope_1d(xf[..., :half], ang_row), rope_1d(xf[..., half:], ang_col)], axis=-1)
    return y.astype(x.dtype)


def attention(q, k, v):
    b, s = q.shape[0], q.shape[1]
    n_blk = s // Q_BLOCK
    grp = N_HEADS // N_KV_HEADS
    qb = q.reshape(b, n_blk, Q_BLOCK, N_KV_HEADS, grp, HEAD_DIM).transpose(1, 0, 2, 3, 4, 5)
    scale = HEAD_DIM ** -0.5

    def one_block(qi):
        sc = jnp.einsum('bqkgd,bskd->bkgqs', qi, k, preferred_element_type=jnp.float32) * scale
        p = jax.nn.softmax(sc, axis=-1).astype(v.dtype)
        return jnp.einsum('bkgqs,bskd->bqkgd', p, v)

    o = lax.map(one_block, qb)
    return o.transpose(1, 0, 2, 3, 4, 5).reshape(b, s, ATTN_WIDTH)


def centred_dwconv(x, w, bias):
    s = x.shape[1]
    xp = jnp.pad(x, ((0, 0), (CONV_PAD_LEFT, CONV_W - 1 - CONV_PAD_LEFT), (0, 0)))
    out = bias
    for tap in range(CONV_W):
        out = out + xp[:, tap:tap + s] * w[tap]
    return out


def _lru_combine(e1, e2):
    a1, b1 = e1
    a2, b2 = e2
    return a1 * a2, a2 * b1 + b2


def rglru_bidir(x, w_gate_a, b_gate_a, w_gate_x, b_gate_x, lru_lambda):
    b, s, _ = x.shape
    xb = x.reshape(b, s, LRU_BLOCKS, LRU_BLOCK_W)
    xf = x.astype(jnp.float32)

    def direction(d, reverse):
        r = jax.nn.sigmoid((jnp.einsum('bsnk,nkj->bsnj', xb, w_gate_a[d]).reshape(b, s, LRU_WIDTH)
                            + b_gate_a[d]).astype(jnp.float32))
        i = jax.nn.sigmoid((jnp.einsum('bsnk,nkj->bsnj', xb, w_gate_x[d]).reshape(b, s, LRU_WIDTH)
                            + b_gate_x[d]).astype(jnp.float32))
        log_a = -LRU_C * r * jax.nn.softplus(-lru_lambda[d].astype(jnp.float32))
        a = jnp.exp(log_a)
        u = jnp.sqrt(-jnp.expm1(2.0 * log_a)) * (i * xf)
        _, h = lax.associative_scan(_lru_combine, (a, u), reverse=reverse, axis=1)
        return h

    return (direction(0, False) + direction(1, True)).astype(x.dtype)


def peer(x, w_query, sub_keys, expert_down, expert_up):
    b, s, d = x.shape
    xt = x.reshape((b * s) // TOKEN_BLOCK, TOKEN_BLOCK, d)

    def one_block(xi):
        q = (xi @ w_query).reshape(TOKEN_BLOCK, PEER_HEADS, 2, PEER_HALF)
        sc = jnp.einsum('thcd,hckd->thck', q, sub_keys, preferred_element_type=jnp.float32)
        top_v, top_i = lax.top_k(sc, PEER_TOPK)
        cand = (top_v[:, :, 0, :, None] + top_v[:, :, 1, None, :]).reshape(TOKEN_BLOCK, PEER_HEADS, PEER_TOPK * PEER_TOPK)
        cand_id = (top_i[:, :, 0, :, None] * N_KEYS + top_i[:, :, 1, None, :]).reshape(TOKEN_BLOCK, PEER_HEADS, PEER_TOPK * PEER_TOPK)
        best, pos = lax.top_k(cand, PEER_TOPK)
        eid = jnp.take_along_axis(cand_id, pos, axis=-1)
        g = jax.nn.softmax(best, axis=-1)
        u = expert_down[eid]
        hid = jax.nn.gelu(jnp.einsum('thkd,td->thk', u, xi, preferred_element_type=jnp.float32), approximate=False)
        wgt = (g * hid).astype(x.dtype)
        return jnp.einsum('thk,thkd->td', wgt, expert_up[eid])

    return lax.map(one_block, xt).reshape(b, s, d)


def encoder_layer(x, ang_row, ang_col, g_mix, w_in, g_q, g_k, conv_w, conv_b,
                  w_gate_a, b_gate_a, w_gate_x, b_gate_x, lru_lambda,
                  g_attn_out, g_lru_out, w_out, g_ffn, w_query, sub_keys, expert_down, expert_up):
    b, s, _ = x.shape
    h = rmsnorm(x, g_mix)
    z = h @ w_in
    cuts = [ATTN_WIDTH, ATTN_WIDTH + KV_WIDTH, ATTN_WIDTH + 2 * KV_WIDTH, ATTN_WIDTH + 2 * KV_WIDTH + LRU_WIDTH]
    q, k, v, xr, yr = jnp.split(z, cuts, axis=-1)
    q = q.reshape(b, s, N_HEADS, HEAD_DIM)
    k = k.reshape(b, s, N_KV_HEADS, HEAD_DIM)
    v = v.reshape(b, s, N_KV_HEADS, HEAD_DIM)
    q = axial_rope(rmsnorm(q, g_q), ang_row, ang_col)
    k = axial_rope(rmsnorm(k, g_k), ang_row, ang_col)
    attn_out = attention(q, k, v)
    xr = centred_dwconv(xr, conv_w, conv_b)
    lru_out = rglru_bidir(xr, w_gate_a, b_gate_a, w_gate_x, b_gate_x, lru_lambda) * jax.nn.gelu(yr, approximate=False)
    mix = jnp.concatenate([rmsnorm(attn_out, g_attn_out), rmsnorm(lru_out, g_lru_out)], axis=-1)
    x = x + mix @ w_out
    x = x + peer(rmsnorm(x, g_ffn), w_query, sub_keys, expert_down, expert_up)
    return x


def trunk(x, g_mix, w_in, g_q, g_k, conv_w, conv_b, w_gate_a, b_gate_a, w_gate_x, b_gate_x, lru_lambda,
          g_attn_out, g_lru_out, w_out, g_ffn, w_query, sub_keys, expert_down, expert_up):
    ang_row, ang_col = axial_angles(x.shape[1])
    for l in range(DEPTH):
        x = encoder_layer(x, ang_row, ang_col, g_mix[l], w_in[l], g_q[l], g_k[l], conv_w[l], conv_b[l],
                          w_gate_a[l], b_gate_a[l], w_gate_x[l], b_gate_x[l], lru_lambda[l],
                          g_attn_out[l], g_lru_out[l], w_out[l], g_ffn[l], w_query[l], sub_keys[l],
                          expert_down[l], expert_up[l])
    return x


def setup_inputs(seed: int = 0) -> dict:
    key = jax.random.key(seed)
    ks = jax.random.split(key, 24)
    f32 = jnp.float32

    def nrm(k, shape, scale):
        return jax.random.normal(k, shape, f32) * scale

    def gain(k, shape):
        return 1.0 + 0.02 * jax.random.normal(k, shape, f32)

    a0 = jax.random.uniform(ks[13], (DEPTH, 2, LRU_WIDTH), f32, 0.9, 0.999)
    p = a0 ** (1.0 / LRU_C)
    lru_lambda = jnp.log(p) - jnp.log1p(-p)
    return {
        'x_prompt': jax.random.normal(ks[0], (BATCH, SEQ, D_MODEL), f32),
        'x_sample': jax.random.normal(ks[1], (DEC_BATCH, DEC_SEQ, D_MODEL), f32),
        'g_mix': gain(ks[2], (DEPTH, D_MODEL)),
        'w_in': nrm(ks[3], (DEPTH, D_MODEL, IN_WIDTH), D_MODEL ** -0.5),
        'g_q': gain(ks[4], (DEPTH, HEAD_DIM)),
        'g_k': gain(ks[5], (DEPTH, HEAD_DIM)),
        'conv_w': nrm(ks[6], (DEPTH, CONV_W, LRU_WIDTH), CONV_W ** -0.5),
        'conv_b': nrm(ks[7], (DEPTH, LRU_WIDTH), 0.01),
        'w_gate_a': nrm(ks[8], (DEPTH, 2, LRU_BLOCKS, LRU_BLOCK_W, LRU_BLOCK_W), LRU_BLOCK_W ** -0.5),
        'b_gate_a': nrm(ks[9], (DEPTH, 2, LRU_WIDTH), 0.01),
        'w_gate_x': nrm(ks[10], (DEPTH, 2, LRU_BLOCKS, LRU_BLOCK_W, LRU_BLOCK_W), LRU_BLOCK_W ** -0.5),
        'b_gate_x': nrm(ks[11], (DEPTH, 2, LRU_WIDTH), 0.01),
        'lru_lambda': lru_lambda,
        'g_attn_out': gain(ks[14], (DEPTH, ATTN_WIDTH)),
        'g_lru_out': gain(ks[15], (DEPTH, LRU_WIDTH)),
        'w_out': nrm(ks[16], (DEPTH, MIX_WIDTH, D_MODEL), MIX_WIDTH ** -0.5),
        'g_ffn': gain(ks[17], (DEPTH, D_MODEL)),
        'w_query': nrm(ks[18], (DEPTH, D_MODEL, PEER_HEADS * PEER_KEY_DIM), D_MODEL ** -0.5),
        'sub_keys': nrm(ks[19], (DEPTH, PEER_HEADS, 2, N_KEYS, PEER_HALF), PEER_HALF ** -0.5),
        'expert_down': nrm(ks[20], (DEPTH, N_EXPERTS, D_MODEL), D_MODEL ** -0.5),
        'expert_up': nrm(ks[21], (DEPTH, N_EXPERTS, D_MODEL), (PEER_HEADS * PEER_TOPK) ** -0.5),
    }


def reference(x_prompt, x_sample, g_mix, w_in, g_q, g_k, conv_w, conv_b, w_gate_a, b_gate_a,
              w_gate_x, b_gate_x, lru_lambda, g_attn_out, g_lru_out, w_out, g_ffn, w_query,
              sub_keys, expert_down, expert_up):
    y_prompt = trunk(x_prompt, g_mix, w_in, g_q, g_k, conv_w, conv_b, w_gate_a, b_gate_a, w_gate_x, b_gate_x,
                     lru_lambda, g_attn_out, g_lru_out, w_out, g_ffn, w_query, sub_keys, expert_down, expert_up)
    y_sample = trunk(x_sample, g_mix, w_in, g_q, g_k, conv_w, conv_b, w_gate_a, b_gate_a, w_gate_x, b_gate_x,
                     lru_lambda, g_attn_out, g_lru_out, w_out, g_ffn, w_query, sub_keys, expert_down, expert_up)
    return (y_prompt, y_sample)
```

```python
import functools
import math

import jax
import jax.numpy as jnp
from jax import lax
from jax.experimental import pallas as pl
from jax.experimental.pallas import tpu as pltpu

F32 = jnp.float32
BF16 = jnp.bfloat16

D_MODEL = 2048
GRID_W = 64
N_HEADS = 8
N_KV_HEADS = 2
HEAD_DIM = 128
HEADS_PER_KV = N_HEADS // N_KV_HEADS
ATTN_WIDTH = N_HEADS * HEAD_DIM
KV_WIDTH = N_KV_HEADS * HEAD_DIM
QKV_WIDTH = ATTN_WIDTH + 2 * KV_WIDTH
ROPE_THETA = 10000.0
LRU_WIDTH = D_MODEL - ATTN_WIDTH
LRU_BLOCK_W = 64
CONV_W = 4
CONV_PAD_LEFT = 2
LRU_C = 8.0
N_KEYS = 128
N_EXPERTS = N_KEYS * N_KEYS
PEER_HEADS = 8
PEER_KEY_DIM = 128
PEER_HALF = PEER_KEY_DIM // 2
PEER_TOPK = 16
EPS = 1e-6

V7X_SUBLANES = 8
V7X_LANES = 128
V7X_MXU_DIM = 256
V7X_VMEM_LIMIT_BYTES = 56 * 1024 * 1024

NEG_INF = float("-inf")


def _params(*semantics):
    return pltpu.CompilerParams(dimension_semantics=semantics,
                                vmem_limit_bytes=V7X_VMEM_LIMIT_BYTES)


def _block(n, target):
    b = min(n, target)
    while n % b:
        b //= 2
    return b


def _gelu(x):
    return 0.5 * x * (1.0 + lax.erf(x * (1.0 / math.sqrt(2.0))))


def _sigmoid(x):
    return 1.0 / (1.0 + jnp.exp(-x))


def _rms(x, g):
    return x * lax.rsqrt(jnp.mean(x * x, axis=-1, keepdims=True) + EPS) * g


def _qkv_kernel(x_ref, gmix_ref, w_ref, gq_ref, gk_ref, cos_ref, sin_ref,
                q_ref, k_ref, v_ref):
    h = _rms(x_ref[...], gmix_ref[...]).astype(BF16)
    z = jnp.dot(h, w_ref[...], preferred_element_type=F32)
    cos = cos_ref[...]
    sin = sin_ref[...]
    lane = lax.broadcasted_iota(jnp.int32, cos.shape, 1)
    first_half = (lane % (HEAD_DIM // 2)) < (HEAD_DIM // 4)
    scale = HEAD_DIM ** -0.5

    def norm_rope(zh, g):
        y = _rms(zh, g)
        partner = jnp.where(first_half,
                            pltpu.roll(y, HEAD_DIM - HEAD_DIM // 4, 1),
                            pltpu.roll(y, HEAD_DIM // 4, 1))
        return y * cos + partner * sin

    gq = gq_ref[...]
    gk = gk_ref[...]
    for hd in range(N_HEADS):
        sl = slice(hd * HEAD_DIM, (hd + 1) * HEAD_DIM)
        q_ref[:, sl] = (norm_rope(z[:, sl], gq) * scale).astype(BF16)
    for hd in range(N_KV_HEADS):
        sl = slice(hd * HEAD_DIM, (hd + 1) * HEAD_DIM)
        k_ref[:, sl] = norm_rope(z[:, ATTN_WIDTH + hd * HEAD_DIM:ATTN_WIDTH + (hd + 1) * HEAD_DIM], gk).astype(BF16)
    v_ref[...] = z[:, ATTN_WIDTH + KV_WIDTH:].astype(BF16)


def _qkv_proj(x2, g_mix, w_qkv, g_q, g_k, cos_t, sin_t, seq):
    n = x2.shape[0]
    tm = _block(seq, 512)
    nseq_blocks = seq // tm
    row = lambda i: (i, 0)
    fixed = lambda i: (0, 0)
    pos = lambda i: (i % nseq_blocks, 0)
    return pl.pallas_call(
        _qkv_kernel,
        grid=(n // tm,),
        in_specs=[
            pl.BlockSpec((tm, D_MODEL), row),
            pl.BlockSpec((1, D_MODEL), fixed),
            pl.BlockSpec((D_MODEL, QKV_WIDTH), fixed),
            pl.BlockSpec((1, HEAD_DIM), fixed),
            pl.BlockSpec((1, HEAD_DIM), fixed),
            pl.BlockSpec((tm, HEAD_DIM), pos),
            pl.BlockSpec((tm, HEAD_DIM), pos),
        ],
        out_specs=[
            pl.BlockSpec((tm, ATTN_WIDTH), row),
            pl.BlockSpec((tm, KV_WIDTH), row),
            pl.BlockSpec((tm, KV_WIDTH), row),
        ],
        out_shape=[
            jax.ShapeDtypeStruct((n, ATTN_WIDTH), BF16),
            jax.ShapeDtypeStruct((n, KV_WIDTH), BF16),
            jax.ShapeDtypeStruct((n, KV_WIDTH), BF16),
        ],
        compiler_params=_params("parallel"),
        name="qkv_proj",
    )(x2, g_mix, w_qkv, g_q, g_k, cos_t, sin_t)


def _xy_kernel(x_ref, gmix_ref, w_ref, xr_ref, yr_ref):
    h = _rms(x_ref[...], gmix_ref[...]).astype(BF16)
    z = jnp.dot(h, w_ref[...], preferred_element_type=F32)
    xr_ref[...] = z[:, :LRU_WIDTH]
    yr_ref[...] = z[:, LRU_WIDTH:]


def _xy_proj(x2, g_mix, w_xy):
    n = x2.shape[0]
    tm = _block(n, 512)
    row = lambda i: (i, 0)
    fixed = lambda i: (0, 0)
    return pl.pallas_call(
        _xy_kernel,
        grid=(n // tm,),
        in_specs=[
            pl.BlockSpec((tm, D_MODEL), row),
            pl.BlockSpec((1, D_MODEL), fixed),
            pl.BlockSpec((D_MODEL, 2 * LRU_WIDTH), fixed),
        ],
        out_specs=[pl.BlockSpec((tm, LRU_WIDTH), row), pl.BlockSpec((tm, LRU_WIDTH), row)],
        out_shape=[jax.ShapeDtypeStruct((n, LRU_WIDTH), F32)] * 2,
        compiler_params=_params("parallel"),
        name="xy_proj",
    )(x2, g_mix, w_xy)


def _attn_kernel(q_ref, k_ref, v_ref, o_ref, m_scr, l_scr, acc_scr):
    ki = pl.program_id(3)

    @pl.when(ki == 0)
    def _():
        m_scr[...] = jnp.full(m_scr.shape, NEG_INF, F32)
        l_scr[...] = jnp.zeros(l_scr.shape, F32)
        acc_scr[...] = jnp.zeros(acc_scr.shape, F32)

    k = k_ref[...]
    v = v_ref[...]
    for hd in range(HEADS_PER_KV):
        q = q_ref[:, hd * HEAD_DIM:(hd + 1) * HEAD_DIM]
        s = lax.dot_general(q, k, (((1,), (1,)), ((), ())), preferred_element_type=F32)
        m_prev = m_scr[hd]
        m_next = jnp.maximum(m_prev, jnp.max(s, axis=1, keepdims=True))
        p = jnp.exp(s - m_next[:, :1])
        alpha = jnp.exp(m_prev - m_next)
        l_scr[hd] = alpha * l_scr[hd] + jnp.sum(p, axis=1, keepdims=True)
        acc_scr[hd] = alpha * acc_scr[hd] + jnp.dot(p.astype(BF16), v, preferred_element_type=F32)
        m_scr[hd] = m_next

    @pl.when(ki == pl.num_programs(3) - 1)
    def _():
        for hd in range(HEADS_PER_KV):
            o_ref[:, hd * HEAD_DIM:(hd + 1) * HEAD_DIM] = acc_scr[hd] / l_scr[hd]


def _attention(q, k, v, batch, seq):
    n = q.shape[0]
    tq = _block(seq, 512)
    tk = _block(seq, 1024)
    nq, nk = seq // tq, seq // tk
    group_w = HEADS_PER_KV * HEAD_DIM
    return pl.pallas_call(
        _attn_kernel,
        grid=(batch, N_KV_HEADS, nq, nk),
        in_specs=[
            pl.BlockSpec((tq, group_w), lambda b, g, qi, ki: (b * nq + qi, g)),
            pl.BlockSpec((tk, HEAD_DIM), lambda b, g, qi, ki: (b * nk + ki, g)),
            pl.BlockSpec((tk, HEAD_DIM), lambda b, g, qi, ki: (b * nk + ki, g)),
        ],
        out_specs=pl.BlockSpec((tq, group_w), lambda b, g, qi, ki: (b * nq + qi, g)),
        out_shape=jax.ShapeDtypeStruct((n, ATTN_WIDTH), F32),
        scratch_shapes=[
            pltpu.VMEM((HEADS_PER_KV, tq, HEAD_DIM), F32),
            pltpu.VMEM((HEADS_PER_KV, tq, HEAD_DIM), F32),
            pltpu.VMEM((HEADS_PER_KV, tq, HEAD_DIM), F32),
        ],
        compiler_params=_params("parallel", "parallel", "parallel", "arbitrary"),
        name="flash_attention",
    )(q, k, v)


def _lru_kernel(reverse, tc, n_chunks, *refs):
    if reverse:
        (xm_ref, xp_ref, xn_ref, cw_ref, cb_ref, wg_ref, bg_ref, lam_ref, hf_ref, yr_ref,
         out_ref, ext_scr, a_scr, u_scr, carry_scr) = refs
    else:
        (xm_ref, xp_ref, xn_ref, cw_ref, cb_ref, wg_ref, bg_ref, lam_ref,
         out_ref, ext_scr, a_scr, u_scr, carry_scr) = refs
    c = pl.program_id(1)
    chunk = (n_chunks - 1 - c) if reverse else c
    halo = V7X_SUBLANES

    @pl.when(c == 0)
    def _():
        carry_scr[...] = jnp.zeros(carry_scr.shape, F32)

    ext_scr[0:halo, :] = jnp.where(chunk == 0, 0.0, xp_ref[...])
    ext_scr[halo:halo + tc, :] = xm_ref[...]
    ext_scr[halo + tc:2 * halo + tc, :] = jnp.where(chunk == n_chunks - 1, 0.0, xn_ref[...])
    xc = cb_ref[...]
    for tap in range(CONV_W):
        start = halo + tap - CONV_PAD_LEFT
        xc = xc + ext_scr[start:start + tc, :] * cw_ref[tap:tap + 1, :]

    xcb = xc.astype(BF16)
    za, zx = [], []
    for g in range(LRU_WIDTH // V7X_MXU_DIM):
        zg = jnp.dot(xcb[:, g * V7X_MXU_DIM:(g + 1) * V7X_MXU_DIM], wg_ref[g],
                     preferred_element_type=F32)
        za.append(zg[:, :V7X_MXU_DIM])
        zx.append(zg[:, V7X_MXU_DIM:])
    r = _sigmoid(jnp.concatenate(za, axis=1) + bg_ref[0:1, :])
    i = _sigmoid(jnp.concatenate(zx, axis=1) + bg_ref[1:2, :])
    lam = lam_ref[...]
    softplus_neg = jnp.maximum(-lam, 0.0) + jnp.log(1.0 + jnp.exp(-jnp.abs(lam)))
    a = jnp.exp((-LRU_C * softplus_neg) * r)
    u = jnp.sqrt(1.0 - a * a) * (i * xc)

    nt = tc // V7X_SUBLANES
    a3 = a.reshape(nt, V7X_SUBLANES, LRU_WIDTH)
    u3 = u.reshape(nt, V7X_SUBLANES, LRU_WIDTH)
    row = lax.broadcasted_iota(jnp.int32, a3.shape, 1)
    for s in (1, 2, 4):
        if reverse:
            shift, valid = V7X_SUBLANES - s, row < V7X_SUBLANES - s
        else:
            shift, valid = s, row >= s
        a_sh = pltpu.roll(a3, shift, 1)
        u_sh = pltpu.roll(u3, shift, 1)
        u3 = u3 + jnp.where(valid, a3 * u_sh, 0.0)
        a3 = jnp.where(valid, a3 * a_sh, a3)
    a_scr[...] = a3
    u_scr[...] = u3

    def tile_step(t, carry):
        tt = (nt - 1 - t) if reverse else t
        h = u_scr[tt] + a_scr[tt] * carry
        u_scr[tt] = h
        edge = h[0:1, :] if reverse else h[V7X_SUBLANES - 1:V7X_SUBLANES, :]
        return jnp.broadcast_to(edge, h.shape)

    carry_scr[...] = lax.fori_loop(0, nt, tile_step, carry_scr[...])
    h_all = u_scr[...].reshape(tc, LRU_WIDTH)
    if reverse:
        out_ref[...] = (h_all + hf_ref[...]) * _gelu(yr_ref[...])
    else:
        out_ref[...] = h_all


def _lru_direction(reverse, xr, conv_w, conv_b, w_gates, b_gates, lam, batch, seq, h_fwd=None, yr=None):
    n = xr.shape[0]
    tc = _block(seq, 256)
    n_chunks = seq // tc
    halo = V7X_SUBLANES
    per_tile = tc // halo
    n_tiles = n // halo

    def chunk_of(c):
        return (n_chunks - 1 - c) if reverse else c

    main = lambda b, c: (b * n_chunks + chunk_of(c), 0)
    prev = lambda b, c: (jnp.maximum((b * n_chunks + chunk_of(c)) * per_tile - 1, 0), 0)
    nxt = lambda b, c: (jnp.minimum((b * n_chunks + chunk_of(c) + 1) * per_tile, n_tiles - 1), 0)
    fixed2 = lambda b, c: (0, 0)
    fixed3 = lambda b, c: (0, 0, 0)
    in_specs = [
        pl.BlockSpec((tc, LRU_WIDTH), main),
        pl.BlockSpec((halo, LRU_WIDTH), prev),
        pl.BlockSpec((halo, LRU_WIDTH), nxt),
        pl.BlockSpec((CONV_W, LRU_WIDTH), fixed2),
        pl.BlockSpec((1, LRU_WIDTH), fixed2),
        pl.BlockSpec((LRU_WIDTH // V7X_MXU_DIM, V7X_MXU_DIM, 2 * V7X_MXU_DIM), fixed3),
        pl.BlockSpec((2, LRU_WIDTH), fixed2),
        pl.BlockSpec((1, LRU_WIDTH), fixed2),
    ]
    args = [xr, xr, xr, conv_w, conv_b, w_gates, b_gates, lam]
    if reverse:
        in_specs += [pl.BlockSpec((tc, LRU_WIDTH), main), pl.BlockSpec((tc, LRU_WIDTH), main)]
        args += [h_fwd, yr]
    return pl.pallas_call(
        functools.partial(_lru_kernel, reverse, tc, n_chunks),
        grid=(batch, n_chunks),
        in_specs=in_specs,
        out_specs=pl.BlockSpec((tc, LRU_WIDTH), main),
        out_shape=jax.ShapeDtypeStruct((n, LRU_WIDTH), F32),
        scratch_shapes=[
            pltpu.VMEM((tc + 2 * halo, LRU_WIDTH), F32),
            pltpu.VMEM((per_tile, halo, LRU_WIDTH), F32),
            pltpu.VMEM((per_tile, halo, LRU_WIDTH), F32),
            pltpu.VMEM((halo, LRU_WIDTH), F32),
        ],
        compiler_params=_params("parallel", "arbitrary"),
        name="lru_bwd" if reverse else "lru_fwd",
    )(*args)


def _out_kernel(x_ref, attn_ref, lru_ref, ga_ref, gl_ref, wa_ref, wl_ref, gf_ref, wq_ref,
                x1_ref, xn_ref, qp_ref):
    an = _rms(attn_ref[...], ga_ref[...]).astype(BF16)
    ln = _rms(lru_ref[...], gl_ref[...]).astype(BF16)
    x1 = (x_ref[...] + jnp.dot(an, wa_ref[...], preferred_element_type=F32)
          + jnp.dot(ln, wl_ref[...], preferred_element_type=F32))
    x1_ref[...] = x1
    xn = _rms(x1, gf_ref[...]).astype(BF16)
    xn_ref[...] = xn
    qp_ref[...] = jnp.dot(xn, wq_ref[...], preferred_element_type=F32)


def _out_proj(x2, attn, lru, g_attn, g_lru, w_out_a, w_out_l, g_ffn, w_query):
    n = x2.shape[0]
    tm = _block(n, 512)
    qw = PEER_HEADS * PEER_KEY_DIM
    row = lambda i: (i, 0)
    fixed = lambda i: (0, 0)
    return pl.pallas_call(
        _out_kernel,
        grid=(n // tm,),
        in_specs=[
            pl.BlockSpec((tm, D_MODEL), row),
            pl.BlockSpec((tm, ATTN_WIDTH), row),
            pl.BlockSpec((tm, LRU_WIDTH), row),
            pl.BlockSpec((1, ATTN_WIDTH), fixed),
            pl.BlockSpec((1, LRU_WIDTH), fixed),
            pl.BlockSpec((ATTN_WIDTH, D_MODEL), fixed),
            pl.BlockSpec((LRU_WIDTH, D_MODEL), fixed),
            pl.BlockSpec((1, D_MODEL), fixed),
            pl.BlockSpec((D_MODEL, qw), fixed),
        ],
        out_specs=[
            pl.BlockSpec((tm, D_MODEL), row),
            pl.BlockSpec((tm, D_MODEL), row),
            pl.BlockSpec((tm, qw), row),
        ],
        out_shape=[
            jax.ShapeDtypeStruct((n, D_MODEL), F32),
            jax.ShapeDtypeStruct((n, D_MODEL), BF16),
            jax.ShapeDtypeStruct((n, qw), F32),
        ],
        compiler_params=_params("parallel"),
        name="out_proj",
    )(x2, attn, lru, g_attn, g_lru, w_out_a, w_out_l, g_ffn, w_query)


def _top_values(s, count):
    vals = []
    for _ in range(count):
        m = jnp.max(s, axis=0, keepdims=True)
        vals.append(m)
        s = jnp.where(s == m, NEG_INF, s)
    return vals


def _peer_score_kernel(qp_ref, keys_ref, s1_ref, s2_ref, e1_ref, e2_ref, tau_ref):
    for hd in range(PEER_HEADS):
        qh = qp_ref[:, hd * PEER_KEY_DIM:(hd + 1) * PEER_KEY_DIM].astype(BF16)
        nt = (((1,), (1,)), ((), ()))
        s1 = lax.dot_general(keys_ref[hd, 0], qh, nt, preferred_element_type=F32)
        s2 = lax.dot_general(keys_ref[hd, 1], qh, nt, preferred_element_type=F32)
        top1 = _top_values(s1, PEER_TOPK)
        top2 = jnp.concatenate(_top_values(s2, PEER_TOPK), axis=0)
        cand = jnp.concatenate([t1 + top2 for t1 in top1], axis=0)
        tau = _top_values(cand, PEER_TOPK)[-1]
        m1 = top1[0]
        m2 = top2[0:1, :]
        z = jnp.sum(jnp.where(cand >= tau, jnp.exp(cand - (m1 + m2)), 0.0), axis=0, keepdims=True)
        s1_ref[hd] = s1
        s2_ref[hd] = s2
        e1_ref[hd] = jnp.exp(s1 - m1) / z
        e2_ref[hd] = jnp.exp(s2 - m2)
        tau_ref[hd] = jnp.broadcast_to(tau, tau_ref.shape[1:])


def _peer_scores(qp, keys_pad):
    n = qp.shape[0]
    tb = _block(n, 256)
    qw = PEER_HEADS * PEER_KEY_DIM
    big = pl.BlockSpec((PEER_HEADS, N_KEYS, tb), lambda i: (0, 0, i))
    big_shape = jax.ShapeDtypeStruct((PEER_HEADS, N_KEYS, n), F32)
    return pl.pallas_call(
        _peer_score_kernel,
        grid=(n // tb,),
        in_specs=[
            pl.BlockSpec((tb, qw), lambda i: (i, 0)),
            pl.BlockSpec((PEER_HEADS, 2, N_KEYS, PEER_KEY_DIM), lambda i: (0, 0, 0, 0)),
        ],
        out_specs=[big, big, big, big,
                   pl.BlockSpec((PEER_HEADS, V7X_SUBLANES, tb), lambda i: (0, 0, i))],
        out_shape=[big_shape, big_shape, big_shape, big_shape,
                   jax.ShapeDtypeStruct((PEER_HEADS, V7X_SUBLANES, n), F32)],
        compiler_params=_params("parallel"),
        name="peer_scores",
    )(qp, keys_pad)


def _peer_mix_kernel(rows_per_chunk, xn_ref, ed_ref, eu_ref, s1_ref, s2_ref, e1_ref, e2_ref, tau_ref,
                     x1_ref, y_ref, w_scr):
    ec = pl.program_id(1)

    @pl.when(ec == 0)
    def _():
        y_ref[...] = x1_ref[...]

    nt = (((1,), (1,)), ((), ()))
    hid = lax.dot_general(ed_ref[...], xn_ref[...], nt, preferred_element_type=F32)
    for il in range(rows_per_chunk):
        i = ec * rows_per_chunk + il
        gate = jnp.zeros((N_KEYS, hid.shape[1]), F32)
        for hd in range(PEER_HEADS):
            s1_row = s1_ref[hd, pl.ds(i, 1), :]
            e1_row = e1_ref[hd, pl.ds(i, 1), :]
            tau_row = tau_ref[hd, 0:1, :]
            sel = (s2_ref[hd] + s1_row) >= tau_row
            gate = gate + jnp.where(sel, e2_ref[hd], 0.0) * e1_row
        rows = slice(il * N_KEYS, (il + 1) * N_KEYS)
        w_scr[rows, :] = (_gelu(hid[rows, :]) * gate).astype(BF16)
    tn = (((0,), (0,)), ((), ()))
    y_ref[...] += lax.dot_general(w_scr[...], eu_ref[...], tn, preferred_element_type=F32)


def _peer_mix(xn, e_down, e_up, s1, s2, e1, e2, tau, x1):
    n = xn.shape[0]
    tb = _block(n, 512)
    ec = 512
    rows_per_chunk = ec // N_KEYS
    tok = lambda t, e: (t, 0)
    exp_ = lambda t, e: (e, 0)
    head = lambda t, e: (0, 0, t)
    big = pl.BlockSpec((PEER_HEADS, N_KEYS, tb), head)
    return pl.pallas_call(
        functools.partial(_peer_mix_kernel, rows_per_chunk),
        grid=(n // tb, N_EXPERTS // ec),
        in_specs=[
            pl.BlockSpec((tb, D_MODEL), tok),
            pl.BlockSpec((ec, D_MODEL), exp_),
            pl.BlockSpec((ec, D_MODEL), exp_),
            big, big, big, big,
            pl.BlockSpec((PEER_HEADS, V7X_SUBLANES, tb), head),
            pl.BlockSpec((tb, D_MODEL), tok),
        ],
        out_specs=pl.BlockSpec((tb, D_MODEL), tok),
        out_shape=jax.ShapeDtypeStruct((n, D_MODEL), F32),
        scratch_shapes=[pltpu.VMEM((ec, tb), BF16)],
        compiler_params=_params("parallel", "arbitrary"),
        name="peer_mix",
    )(xn, e_down, e_up, s1, s2, e1, e2, tau, x1)


def _rope_tables(seq):
    t = jnp.arange(seq, dtype=jnp.int32)
    row = (t // GRID_W).astype(F32)
    col = (t % GRID_W).astype(F32)
    half = HEAD_DIM // 2
    inv = ROPE_THETA ** (-jnp.arange(0, half, 2, dtype=F32) / half)
    ar = row[:, None] * inv
    ac = col[:, None] * inv
    cos_t = jnp.concatenate([jnp.cos(ar), jnp.cos(ar), jnp.cos(ac), jnp.cos(ac)], axis=-1)
    sin_t = jnp.concatenate([-jnp.sin(ar), jnp.sin(ar), -jnp.sin(ac), jnp.sin(ac)], axis=-1)
    return cos_t, sin_t


def _gate_weights(w_gate_a, w_gate_x):
    per_tile = V7X_MXU_DIM // LRU_BLOCK_W
    n_tiles = LRU_WIDTH // V7X_MXU_DIM

    def tiles(w):
        w = w.reshape(n_tiles, per_tile, LRU_BLOCK_W, LRU_BLOCK_W)
        eye = jnp.eye(per_tile, dtype=w.dtype)
        return jnp.einsum("tpkj,pq->tpkqj", w, eye).reshape(n_tiles, V7X_MXU_DIM, V7X_MXU_DIM)

    return jnp.concatenate([tiles(w_gate_a), tiles(w_gate_x)], axis=-1).astype(BF16)


def _padded_keys(sub_keys):
    z = jnp.zeros_like(sub_keys[:, 0])
    k0 = jnp.concatenate([sub_keys[:, 0], z], axis=-1)
    k1 = jnp.concatenate([z, sub_keys[:, 1]], axis=-1)
    return jnp.stack([k0, k1], axis=1).astype(BF16)


def _layer(x2, p, batch, seq, cos_t, sin_t):
    q, k, v = _qkv_proj(x2, p["g_mix"], p["w_qkv"], p["g_q"], p["g_k"], cos_t, sin_t, seq)
    attn = _attention(q, k, v, batch, seq)
    xr, yr = _xy_proj(x2, p["g_mix"], p["w_xy"])
    h_fwd = _lru_direction(False, xr, p["conv_w"], p["conv_b"], p["w_gates"][0], p["b_gates"][0],
                           p["lam"][0], batch, seq)
    lru = _lru_direction(True, xr, p["conv_w"], p["conv_b"], p["w_gates"][1], p["b_gates"][1],
                         p["lam"][1], batch, seq, h_fwd=h_fwd, yr=yr)
    x1, xn, qp = _out_proj(x2, attn, lru, p["g_attn_out"], p["g_lru_out"], p["w_out_a"], p["w_out_l"],
                           p["g_ffn"], p["w_query"])
    s1, s2, e1, e2, tau = _peer_scores(qp, p["keys_pad"])
    return _peer_mix(xn, p["e_down"], p["e_up"], s1, s2, e1, e2, tau, x1)


def _trunk(x, layers):
    batch, seq, _ = x.shape
    x2 = x.reshape(batch * seq, D_MODEL)
    cos_t, sin_t = _rope_tables(seq)
    for p in layers:
        x2 = _layer(x2, p, batch, seq, cos_t, sin_t)
    return x2.reshape(batch, seq, D_MODEL)


def _layer_params(l, g_mix, w_in, g_q, g_k, conv_w, conv_b, w_gate_a, b_gate_a, w_gate_x, b_gate_x, lru_lambda,
                  g_attn_out, g_lru_out, w_out, g_ffn, w_query, sub_keys, expert_down, expert_up):
    w_in_b = w_in[l].astype(BF16)
    w_out_b = w_out[l].astype(BF16)
    return {
        "g_mix": g_mix[l][None, :],
        "w_qkv": w_in_b[:, :QKV_WIDTH],
        "w_xy": w_in_b[:, QKV_WIDTH:],
        "g_q": g_q[l][None, :],
        "g_k": g_k[l][None, :],
        "conv_w": conv_w[l],
        "conv_b": conv_b[l][None, :],
        "w_gates": [_gate_weights(w_gate_a[l, d], w_gate_x[l, d]) for d in range(2)],
        "b_gates": [jnp.stack([b_gate_a[l, d], b_gate_x[l, d]], axis=0) for d in range(2)],
        "lam": [lru_lambda[l, d][None, :] for d in range(2)],
        "g_attn_out": g_attn_out[l][None, :],
        "g_lru_out": g_lru_out[l][None, :],
        "w_out_a": w_out_b[:ATTN_WIDTH],
        "w_out_l": w_out_b[ATTN_WIDTH:],
        "g_ffn": g_ffn[l][None, :],
        "w_query": w_query[l].astype(BF16),
        "keys_pad": _padded_keys(sub_keys[l]),
        "e_down": expert_down[l].astype(BF16),
        "e_up": expert_up[l].astype(BF16),
    }


def kernel(x_prompt, x_sample, g_mix, w_in, g_q, g_k, conv_w, conv_b, w_gate_a, b_gate_a, w_gate_x, b_gate_x, lru_lambda, g_attn_out, g_lru_out, w_out, g_ffn, w_query, sub_keys, expert_down, expert_up):
    weights = (g_mix, w_in, g_q, g_k, conv_w, conv_b, w_gate_a, b_gate_a, w_gate_x, b_gate_x, lru_lambda,
               g_attn_out, g_lru_out, w_out, g_ffn, w_query, sub_keys, expert_down, expert_up)
    layers = [_layer_params(l, *weights) for l in range(w_in.shape[0])]
    return (_trunk(x_prompt, layers), _trunk(x_sample, layers))
```

```python
import functools
import math

import jax
import jax.numpy as jnp
from jax import lax
from jax.experimental import pallas as pl
from jax.experimental.pallas import tpu as pltpu

F32 = jnp.float32
BF16 = jnp.bfloat16

D_MODEL = 2048
GRID_W = 64
N_HEADS = 8
N_KV_HEADS = 2
HEAD_DIM = 128
HEADS_PER_KV = N_HEADS // N_KV_HEADS
ATTN_WIDTH = N_HEADS * HEAD_DIM
KV_WIDTH = N_KV_HEADS * HEAD_DIM
QKV_WIDTH = ATTN_WIDTH + 2 * KV_WIDTH
ROPE_THETA = 10000.0
LRU_WIDTH = D_MODEL - ATTN_WIDTH
LRU_BLOCK_W = 64
CONV_W = 4
CONV_PAD_LEFT = 2
LRU_C = 8.0
N_KEYS = 128
N_EXPERTS = N_KEYS * N_KEYS
PEER_HEADS = 8
PEER_KEY_DIM = 128
PEER_HALF = PEER_KEY_DIM // 2
PEER_TOPK = 16
EPS = 1e-6

V7X_SUBLANES = 8
V7X_LANES = 128
V7X_MXU_DIM = 256
V7X_VMEM_LIMIT_BYTES = 56 * 1024 * 1024

NEG_INF = float("-inf")


def _params(*semantics, flags=None):
    return pltpu.CompilerParams(dimension_semantics=semantics,
                                vmem_limit_bytes=V7X_VMEM_LIMIT_BYTES,
                                flags=flags)


def _block(n, target):
    b = min(n, target)
    while n % b:
        b //= 2
    return b


def _gelu(x):
    return 0.5 * x * (1.0 + lax.erf(x * (1.0 / math.sqrt(2.0))))


def _sigmoid(x):
    return 1.0 / (1.0 + jnp.exp(-x))


def _rms(x, g):
    return x * lax.rsqrt(jnp.mean(x * x, axis=-1, keepdims=True) + EPS) * g


def _qkv_kernel(x_ref, gmix_ref, w_ref, gq_ref, gk_ref, cos_ref, sin_ref,
                q_ref, k_ref, v_ref):
    h = _rms(x_ref[...], gmix_ref[...]).astype(BF16)
    z = jnp.dot(h, w_ref[...], preferred_element_type=F32)
    cos = cos_ref[...]
    sin = sin_ref[...]
    lane = lax.broadcasted_iota(jnp.int32, cos.shape, 1)
    first_half = (lane % (HEAD_DIM // 2)) < (HEAD_DIM // 4)
    scale = HEAD_DIM ** -0.5 * math.log2(math.e)

    def norm_rope(zh, g):
        y = _rms(zh, g)
        partner = jnp.where(first_half,
                            pltpu.roll(y, HEAD_DIM - HEAD_DIM // 4, 1),
                            pltpu.roll(y, HEAD_DIM // 4, 1))
        return y * cos + partner * sin

    gq = gq_ref[...]
    gk = gk_ref[...]
    for hd in range(N_HEADS):
        sl = slice(hd * HEAD_DIM, (hd + 1) * HEAD_DIM)
        q_ref[:, sl] = (norm_rope(z[:, sl], gq) * scale).astype(BF16)
    for hd in range(N_KV_HEADS):
        sl = slice(hd * HEAD_DIM, (hd + 1) * HEAD_DIM)
        k_ref[:, sl] = norm_rope(z[:, ATTN_WIDTH + hd * HEAD_DIM:ATTN_WIDTH + (hd + 1) * HEAD_DIM], gk).astype(BF16)
    ones = jnp.ones((z.shape[0], HEAD_DIM), BF16)
    for hd in range(N_KV_HEADS):
        v0 = ATTN_WIDTH + KV_WIDTH + hd * HEAD_DIM
        v_ref[:, 2 * hd * HEAD_DIM:(2 * hd + 1) * HEAD_DIM] = z[:, v0:v0 + HEAD_DIM].astype(BF16)
        v_ref[:, (2 * hd + 1) * HEAD_DIM:(2 * hd + 2) * HEAD_DIM] = ones


def _qkv_proj(x2, g_mix, w_qkv, g_q, g_k, cos_t, sin_t, seq):
    n = x2.shape[0]
    tm = _block(seq, 512)
    nseq_blocks = seq // tm
    row = lambda i: (i, 0)
    fixed = lambda i: (0, 0)
    pos = lambda i: (i % nseq_blocks, 0)
    return pl.pallas_call(
        _qkv_kernel,
        grid=(n // tm,),
        in_specs=[
            pl.BlockSpec((tm, D_MODEL), row),
            pl.BlockSpec((1, D_MODEL), fixed),
            pl.BlockSpec((D_MODEL, QKV_WIDTH), fixed),
            pl.BlockSpec((1, HEAD_DIM), fixed),
            pl.BlockSpec((1, HEAD_DIM), fixed),
            pl.BlockSpec((tm, HEAD_DIM), pos),
            pl.BlockSpec((tm, HEAD_DIM), pos),
        ],
        out_specs=[
            pl.BlockSpec((tm, ATTN_WIDTH), row),
            pl.BlockSpec((tm, KV_WIDTH), row),
            pl.BlockSpec((tm, 2 * KV_WIDTH), row),
        ],
        out_shape=[
            jax.ShapeDtypeStruct((n, ATTN_WIDTH), BF16),
            jax.ShapeDtypeStruct((n, KV_WIDTH), BF16),
            jax.ShapeDtypeStruct((n, 2 * KV_WIDTH), BF16),
        ],
        compiler_params=_params("parallel"),
        name="qkv_proj",
    )(x2, g_mix, w_qkv, g_q, g_k, cos_t, sin_t)


def _xy_kernel(x_ref, gmix_ref, w_ref, xr_ref, yr_ref):
    h = _rms(x_ref[...], gmix_ref[...]).astype(BF16)
    z = jnp.dot(h, w_ref[...], preferred_element_type=F32)
    xr_ref[...] = z[:, :LRU_WIDTH]
    yr_ref[...] = z[:, LRU_WIDTH:]


def _xy_proj(x2, g_mix, w_xy):
    n = x2.shape[0]
    tm = _block(n, 512)
    row = lambda i: (i, 0)
    fixed = lambda i: (0, 0)
    return pl.pallas_call(
        _xy_kernel,
        grid=(n // tm,),
        in_specs=[
            pl.BlockSpec((tm, D_MODEL), row),
            pl.BlockSpec((1, D_MODEL), fixed),
            pl.BlockSpec((D_MODEL, 2 * LRU_WIDTH), fixed),
        ],
        out_specs=[pl.BlockSpec((tm, LRU_WIDTH), row), pl.BlockSpec((tm, LRU_WIDTH), row)],
        out_shape=[jax.ShapeDtypeStruct((n, LRU_WIDTH), F32)] * 2,
        compiler_params=_params("parallel"),
        name="xy_proj",
    )(x2, g_mix, w_xy)


def _attn_kernel(q_ref, k_ref, v_ref, o_ref, m_scr, acc_scr):
    ki = pl.program_id(3)

    @pl.when(ki == 0)
    def _():
        m_scr[...] = jnp.full(m_scr.shape, NEG_INF, F32)
        acc_scr[...] = jnp.zeros(acc_scr.shape, F32)

    k = k_ref[...]
    v = v_ref[...]
    tk = k.shape[0]
    tq = q_ref.shape[0]
    lane_tiles = tk // HEAD_DIM
    nt = (((1,), (1,)), ((), ()))
    for hd in range(HEADS_PER_KV):
        q = q_ref[:, hd * HEAD_DIM:(hd + 1) * HEAD_DIM]
        if hd == 0 and tk % (2 * V7X_MXU_DIM) == 0:
            s = jnp.concatenate([lax.dot_general(q, k[:tk // 2], nt, preferred_element_type=F32),
                                 lax.dot_general(q, k[tk // 2:], nt, preferred_element_type=F32)], axis=1)
        else:
            s = lax.dot_general(q, k, nt, preferred_element_type=F32)
        m_prev = m_scr[hd]
        m_next = jnp.maximum(m_prev, jnp.max(s, axis=1, keepdims=True))
        p = jnp.exp2(s - jnp.tile(m_next, (1, lane_tiles))).astype(BF16)
        alpha = jnp.exp2(m_prev - m_next)
        if hd == HEADS_PER_KV - 1 and tq % 32 == 0:
            pv = jnp.concatenate([jnp.dot(p[:tq // 2], v, preferred_element_type=F32),
                                  jnp.dot(p[tq // 2:], v, preferred_element_type=F32)], axis=0)
        else:
            pv = jnp.dot(p, v, preferred_element_type=F32)
        acc_scr[hd] = jnp.tile(alpha, (1, 2)) * acc_scr[hd] + pv
        m_scr[hd] = m_next

    @pl.when(ki == pl.num_programs(3) - 1)
    def _():
        for hd in range(HEADS_PER_KV):
            acc = acc_scr[hd]
            o_ref[:, hd * HEAD_DIM:(hd + 1) * HEAD_DIM] = acc[:, :HEAD_DIM] / acc[:, HEAD_DIM:]


def _attention(q, k, v, batch, seq):
    n = q.shape[0]
    tq = _block(seq, 512)
    tk = _block(seq, 2048)
    nq, nk = seq // tq, seq // tk
    group_w = HEADS_PER_KV * HEAD_DIM
    return pl.pallas_call(
        _attn_kernel,
        grid=(batch, N_KV_HEADS, nq, nk),
        in_specs=[
            pl.BlockSpec((tq, group_w), lambda b, g, qi, ki: (b * nq + qi, g)),
            pl.BlockSpec((tk, HEAD_DIM), lambda b, g, qi, ki: (b * nk + ki, g)),
            pl.BlockSpec((tk, 2 * HEAD_DIM), lambda b, g, qi, ki: (b * nk + ki, g)),
        ],
        out_specs=pl.BlockSpec((tq, group_w), lambda b, g, qi, ki: (b * nq + qi, g)),
        out_shape=jax.ShapeDtypeStruct((n, ATTN_WIDTH), F32),
        scratch_shapes=[
            pltpu.VMEM((HEADS_PER_KV, tq, HEAD_DIM), F32),
            pltpu.VMEM((HEADS_PER_KV, tq, 2 * HEAD_DIM), F32),
        ],
        compiler_params=_params("parallel", "parallel", "parallel", "arbitrary"),
        name="flash_attention",
    )(q, k, v)


def _lru_kernel(reverse, tc, n_chunks, *refs):
    if reverse:
        (xm_ref, xp_ref, xn_ref, cw_ref, cb_ref, wg_ref, bg_ref, lam_ref, hf_ref, yr_ref,
         out_ref, ext_scr, a_scr, u_scr, carry_scr) = refs
    else:
        (xm_ref, xp_ref, xn_ref, cw_ref, cb_ref, wg_ref, bg_ref, lam_ref,
         out_ref, ext_scr, a_scr, u_scr, carry_scr) = refs
    c = pl.program_id(1)
    chunk = (n_chunks - 1 - c) if reverse else c
    halo = V7X_SUBLANES

    @pl.when(c == 0)
    def _():
        carry_scr[...] = jnp.zeros(carry_scr.shape, F32)

    ext_scr[0:halo, :] = jnp.where(chunk == 0, 0.0, xp_ref[...])
    ext_scr[halo:halo + tc, :] = xm_ref[...]
    ext_scr[halo + tc:2 * halo + tc, :] = jnp.where(chunk == n_chunks - 1, 0.0, xn_ref[...])
    xc = cb_ref[...]
    for tap in range(CONV_W):
        start = halo + tap - CONV_PAD_LEFT
        xc = xc + ext_scr[start:start + tc, :] * cw_ref[tap:tap + 1, :]

    xcb = xc.astype(BF16)
    za, zx = [], []
    for g in range(LRU_WIDTH // V7X_MXU_DIM):
        zg = jnp.dot(xcb[:, g * V7X_MXU_DIM:(g + 1) * V7X_MXU_DIM], wg_ref[g],
                     preferred_element_type=F32)
        za.append(zg[:, :V7X_MXU_DIM])
        zx.append(zg[:, V7X_MXU_DIM:])
    r = _sigmoid(jnp.concatenate(za, axis=1) + bg_ref[0:1, :])
    i = _sigmoid(jnp.concatenate(zx, axis=1) + bg_ref[1:2, :])
    lam = lam_ref[...]
    softplus_neg = jnp.maximum(-lam, 0.0) + jnp.log(1.0 + jnp.exp(-jnp.abs(lam)))
    a = jnp.exp((-LRU_C * softplus_neg) * r)
    u = jnp.sqrt(1.0 - a * a) * (i * xc)

    nt = tc // V7X_SUBLANES
    a3 = a.reshape(nt, V7X_SUBLANES, LRU_WIDTH)
    u3 = u.reshape(nt, V7X_SUBLANES, LRU_WIDTH)
    row = lax.broadcasted_iota(jnp.int32, a3.shape, 1)
    for s in (1, 2, 4):
        if reverse:
            shift, valid = V7X_SUBLANES - s, row < V7X_SUBLANES - s
        else:
            shift, valid = s, row >= s
        a_sh = pltpu.roll(a3, shift, 1)
        u_sh = pltpu.roll(u3, shift, 1)
        u3 = u3 + jnp.where(valid, a3 * u_sh, 0.0)
        a3 = jnp.where(valid, a3 * a_sh, a3)
    a_scr[...] = a3
    u_scr[...] = u3

    def tile_step(t, carry):
        tt = (nt - 1 - t) if reverse else t
        h = u_scr[tt] + a_scr[tt] * carry
        u_scr[tt] = h
        edge = h[0:1, :] if reverse else h[V7X_SUBLANES - 1:V7X_SUBLANES, :]
        return jnp.broadcast_to(edge, h.shape)

    carry_scr[...] = lax.fori_loop(0, nt, tile_step, carry_scr[...])
    h_all = u_scr[...].reshape(tc, LRU_WIDTH)
    if reverse:
        out_ref[...] = (h_all + hf_ref[...]) * _gelu(yr_ref[...])
    else:
        out_ref[...] = h_all


def _lru_direction(reverse, xr, conv_w, conv_b, w_gates, b_gates, lam, batch, seq, h_fwd=None, yr=None):
    n = xr.shape[0]
    tc = _block(seq, 256)
    n_chunks = seq // tc
    halo = V7X_SUBLANES
    per_tile = tc // halo
    n_tiles = n // halo

    def chunk_of(c):
        return (n_chunks - 1 - c) if reverse else c

    main = lambda b, c: (b * n_chunks + chunk_of(c), 0)
    prev = lambda b, c: (jnp.maximum((b * n_chunks + chunk_of(c)) * per_tile - 1, 0), 0)
    nxt = lambda b, c: (jnp.minimum((b * n_chunks + chunk_of(c) + 1) * per_tile, n_tiles - 1), 0)
    fixed2 = lambda b, c: (0, 0)
    fixed3 = lambda b, c: (0, 0, 0)
    in_specs = [
        pl.BlockSpec((tc, LRU_WIDTH), main),
        pl.BlockSpec((halo, LRU_WIDTH), prev),
        pl.BlockSpec((halo, LRU_WIDTH), nxt),
        pl.BlockSpec((CONV_W, LRU_WIDTH), fixed2),
        pl.BlockSpec((1, LRU_WIDTH), fixed2),
        pl.BlockSpec((LRU_WIDTH // V7X_MXU_DIM, V7X_MXU_DIM, 2 * V7X_MXU_DIM), fixed3),
        pl.BlockSpec((2, LRU_WIDTH), fixed2),
        pl.BlockSpec((1, LRU_WIDTH), fixed2),
    ]
    args = [xr, xr, xr, conv_w, conv_b, w_gates, b_gates, lam]
    if reverse:
        in_specs += [pl.BlockSpec((tc, LRU_WIDTH), main), pl.BlockSpec((tc, LRU_WIDTH), main)]
        args += [h_fwd, yr]
    return pl.pallas_call(
        functools.partial(_lru_kernel, reverse, tc, n_chunks),
        grid=(batch, n_chunks),
        in_specs=in_specs,
        out_specs=pl.BlockSpec((tc, LRU_WIDTH), main),
        out_shape=jax.ShapeDtypeStruct((n, LRU_WIDTH), F32),
        scratch_shapes=[
            pltpu.VMEM((tc + 2 * halo, LRU_WIDTH), F32),
            pltpu.VMEM((per_tile, halo, LRU_WIDTH), F32),
            pltpu.VMEM((per_tile, halo, LRU_WIDTH), F32),
            pltpu.VMEM((halo, LRU_WIDTH), F32),
        ],
        compiler_params=_params("parallel", "arbitrary"),
        name="lru_bwd" if reverse else "lru_fwd",
    )(*args)


def _out_kernel(x_ref, attn_ref, lru_ref, ga_ref, gl_ref, wa_ref, wl_ref, gf_ref, wq_ref,
                x1_ref, xnt_ref, qp_ref):
    an = _rms(attn_ref[...], ga_ref[...]).astype(BF16)
    ln = _rms(lru_ref[...], gl_ref[...]).astype(BF16)
    x1 = (x_ref[...] + jnp.dot(an, wa_ref[...], preferred_element_type=F32)
          + jnp.dot(ln, wl_ref[...], preferred_element_type=F32))
    x1_ref[...] = x1
    xn = _rms(x1, gf_ref[...])
    xnt_ref[...] = xn.T.astype(BF16)
    qp_ref[...] = jnp.dot(xn.astype(BF16), wq_ref[...], preferred_element_type=F32)


def _out_proj(x2, attn, lru, g_attn, g_lru, w_out_a, w_out_l, g_ffn, w_query):
    n = x2.shape[0]
    tm = _block(n, 512)
    qw = PEER_HEADS * PEER_KEY_DIM
    row = lambda i: (i, 0)
    fixed = lambda i: (0, 0)
    return pl.pallas_call(
        _out_kernel,
        grid=(n // tm,),
        in_specs=[
            pl.BlockSpec((tm, D_MODEL), row),
            pl.BlockSpec((tm, ATTN_WIDTH), row),
            pl.BlockSpec((tm, LRU_WIDTH), row),
            pl.BlockSpec((1, ATTN_WIDTH), fixed),
            pl.BlockSpec((1, LRU_WIDTH), fixed),
            pl.BlockSpec((ATTN_WIDTH, D_MODEL), fixed),
            pl.BlockSpec((LRU_WIDTH, D_MODEL), fixed),
            pl.BlockSpec((1, D_MODEL), fixed),
            pl.BlockSpec((D_MODEL, qw), fixed),
        ],
        out_specs=[
            pl.BlockSpec((tm, D_MODEL), row),
            pl.BlockSpec((D_MODEL, tm), lambda i: (0, i)),
            pl.BlockSpec((tm, qw), row),
        ],
        out_shape=[
            jax.ShapeDtypeStruct((n, D_MODEL), F32),
            jax.ShapeDtypeStruct((D_MODEL, n), BF16),
            jax.ShapeDtypeStruct((n, qw), F32),
        ],
        compiler_params=_params("parallel"),
        name="out_proj",
    )(x2, attn, lru, g_attn, g_lru, w_out_a, w_out_l, g_ffn, w_query)


def _top_values(s, count, with_rank=False):
    vals = []
    rank = jnp.full(s.shape, float(count), F32) if with_rank else None
    for r in range(count):
        m = jnp.max(s, axis=0, keepdims=True)
        vals.append(m)
        hit = s == m
        if with_rank:
            rank = jnp.minimum(rank, jnp.where(hit, float(r), float(count)))
        s = jnp.where(hit, NEG_INF, s)
    return (vals, rank) if with_rank else vals


def _peer_score_kernel(qp_ref, keys_ref, n1_ref, e1_ref, c2_ref, e2_ref):
    half_tile = V7X_SUBLANES
    sub = lax.broadcasted_iota(jnp.int32, (half_tile, qp_ref.shape[0]), 0)
    for hd in range(PEER_HEADS):
        qh = qp_ref[:, hd * PEER_KEY_DIM:(hd + 1) * PEER_KEY_DIM].astype(BF16)
        nt = (((1,), (1,)), ((), ()))
        s1 = lax.dot_general(keys_ref[hd, 0], qh, nt, preferred_element_type=F32)
        s2 = lax.dot_general(keys_ref[hd, 1], qh, nt, preferred_element_type=F32)
        top1 = _top_values(s1, PEER_TOPK)
        top2_rows, rank2 = _top_values(s2, PEER_TOPK, with_rank=True)
        top2 = jnp.concatenate(top2_rows, axis=0)
        cands = [top1[0] + top2]
        for r in range(1, PEER_TOPK):
            ok = sub < PEER_TOPK // (r + 1)
            cands.append(jnp.where(ok, top1[r] + top2[:half_tile, :], NEG_INF))
        tau = _top_values(jnp.concatenate(cands, axis=0), PEER_TOPK)[-1]
        m12 = top1[0] + top2[0:1, :]
        z = jnp.zeros_like(tau)
        n1 = jnp.zeros_like(s1)
        for r in range(PEER_TOPK):
            picked = cands[r] >= tau
            z = z + jnp.sum(jnp.where(picked, jnp.exp(cands[r] - m12), 0.0), axis=0, keepdims=True)
            n_r = jnp.sum(jnp.where(picked, 1.0, 0.0), axis=0, keepdims=True)
            n1 = jnp.where(s1 == top1[r], n_r, n1)
        n1_ref[hd] = n1
        e1_ref[hd] = jnp.exp(s1 - top1[0]) / z
        c2_ref[hd] = rank2.astype(BF16)
        e2_ref[hd] = jnp.exp(s2 - top2[0:1, :]).astype(BF16)


def _peer_scores(qp, keys_pad):
    n = qp.shape[0]
    tb = _block(n, 256)
    qw = PEER_HEADS * PEER_KEY_DIM
    big = pl.BlockSpec((PEER_HEADS, N_KEYS, tb), lambda i: (0, 0, i))
    shape_f32 = jax.ShapeDtypeStruct((PEER_HEADS, N_KEYS, n), F32)
    shape_bf16 = jax.ShapeDtypeStruct((PEER_HEADS, N_KEYS, n), BF16)
    return pl.pallas_call(
        _peer_score_kernel,
        grid=(n // tb,),
        in_specs=[
            pl.BlockSpec((tb, qw), lambda i: (i, 0)),
            pl.BlockSpec((PEER_HEADS, 2, N_KEYS, PEER_KEY_DIM), lambda i: (0, 0, 0, 0)),
        ],
        out_specs=[big, big, big, big],
        out_shape=[shape_f32, shape_f32, shape_bf16, shape_bf16],
        compiler_params=_params("parallel"),
        name="peer_scores",
    )(qp, keys_pad)


PEER_KEYS_PER_STEP = V7X_SUBLANES
PEER_EXPERTS_PER_STEP = PEER_KEYS_PER_STEP * N_KEYS


def _peer_gate(n1_ref, e1_ref, c2_ref, e2_ref, g_scr):
    tb = g_scr.shape[1]
    bf16_rows = 2 * V7X_SUBLANES

    def key_row(ref, hd, il):
        row = jnp.broadcast_to(ref[hd, il:il + 1, :], (bf16_rows, tb)).astype(BF16)
        return jnp.tile(row, (N_KEYS // bf16_rows, 1))

    for il in range(PEER_KEYS_PER_STEP):
        gate = jnp.zeros((N_KEYS, tb), BF16)
        for hd in range(PEER_HEADS):
            n_row = key_row(n1_ref, hd, il)
            gate = gate + jnp.where(c2_ref[hd] < n_row, e2_ref[hd], 0.0) * key_row(e1_ref, hd, il)
        g_scr[il * N_KEYS:(il + 1) * N_KEYS, :] = gate


def _peer_mix_kernel(xnt_ref, ed_ref, eu_ref, n1_ref, e1_ref, c2_ref, e2_ref, x1_ref, y_ref, g_scr, w_scr):
    ec = pl.program_id(1)

    @pl.when(ec == 0)
    def _():
        y_ref[...] = x1_ref[...]

    _peer_gate(n1_ref, e1_ref, c2_ref, e2_ref, g_scr)
    hid = jnp.dot(ed_ref[...], xnt_ref[...], preferred_element_type=F32)
    w_scr[...] = _gelu(hid).astype(BF16) * g_scr[...]
    tn = (((0,), (0,)), ((), ()))
    y_ref[...] += lax.dot_general(w_scr[...], eu_ref[...], tn, preferred_element_type=F32)


def _peer_mix(xnt, e_down, e_up, n1, e1, c2, e2, x1):
    n = xnt.shape[1]
    tb = _block(n, 512)
    ec = PEER_EXPERTS_PER_STEP
    tok = lambda t, e: (t, 0)
    exp_ = lambda t, e: (e, 0)
    key1 = pl.BlockSpec((PEER_HEADS, PEER_KEYS_PER_STEP, tb), lambda t, e: (0, e, t))
    key2 = pl.BlockSpec((PEER_HEADS, N_KEYS, tb), lambda t, e: (0, 0, t))
    return pl.pallas_call(
        _peer_mix_kernel,
        grid=(n // tb, N_EXPERTS // ec),
        in_specs=[
            pl.BlockSpec((D_MODEL, tb), lambda t, e: (0, t)),
            pl.BlockSpec((ec, D_MODEL), exp_),
            pl.BlockSpec((ec, D_MODEL), exp_),
            key1, key1, key2, key2,
            pl.BlockSpec((tb, D_MODEL), tok),
        ],
        out_specs=pl.BlockSpec((tb, D_MODEL), tok),
        out_shape=jax.ShapeDtypeStruct((n, D_MODEL), F32),
        scratch_shapes=[pltpu.VMEM((ec, tb), BF16), pltpu.VMEM((ec, tb), BF16)],
        compiler_params=_params("parallel", "arbitrary"),
        name="peer_mix",
    )(xnt, e_down, e_up, n1, e1, c2, e2, x1)


def _rope_tables(seq):
    t = jnp.arange(seq, dtype=jnp.int32)
    row = (t // GRID_W).astype(F32)
    col = (t % GRID_W).astype(F32)
    half = HEAD_DIM // 2
    inv = ROPE_THETA ** (-jnp.arange(0, half, 2, dtype=F32) / half)
    ar = row[:, None] * inv
    ac = col[:, None] * inv
    cos_t = jnp.concatenate([jnp.cos(ar), jnp.cos(ar), jnp.cos(ac), jnp.cos(ac)], axis=-1)
    sin_t = jnp.concatenate([-jnp.sin(ar), jnp.sin(ar), -jnp.sin(ac), jnp.sin(ac)], axis=-1)
    return cos_t, sin_t


def _gate_weights(w_gate_a, w_gate_x):
    per_tile = V7X_MXU_DIM // LRU_BLOCK_W
    n_tiles = LRU_WIDTH // V7X_MXU_DIM

    def tiles(w):
        w = w.reshape(n_tiles, per_tile, LRU_BLOCK_W, LRU_BLOCK_W)
        eye = jnp.eye(per_tile, dtype=w.dtype)
        return jnp.einsum("tpkj,pq->tpkqj", w, eye).reshape(n_tiles, V7X_MXU_DIM, V7X_MXU_DIM)

    return jnp.concatenate([tiles(w_gate_a), tiles(w_gate_x)], axis=-1).astype(BF16)


def _padded_keys(sub_keys):
    z = jnp.zeros_like(sub_keys[:, 0])
    k0 = jnp.concatenate([sub_keys[:, 0], z], axis=-1)
    k1 = jnp.concatenate([z, sub_keys[:, 1]], axis=-1)
    return jnp.stack([k0, k1], axis=1).astype(BF16)


def _layer(x2, p, batch, seq, cos_t, sin_t):
    q, k, v = _qkv_proj(x2, p["g_mix"], p["w_qkv"], p["g_q"], p["g_k"], cos_t, sin_t, seq)
    attn = _attention(q, k, v, batch, seq)
    xr, yr = _xy_proj(x2, p["g_mix"], p["w_xy"])
    h_fwd = _lru_direction(False, xr, p["conv_w"], p["conv_b"], p["w_gates"][0], p["b_gates"][0],
                           p["lam"][0], batch, seq)
    lru = _lru_direction(True, xr, p["conv_w"], p["conv_b"], p["w_gates"][1], p["b_gates"][1],
                         p["lam"][1], batch, seq, h_fwd=h_fwd, yr=yr)
    x1, xnt, qp = _out_proj(x2, attn, lru, p["g_attn_out"], p["g_lru_out"], p["w_out_a"], p["w_out_l"],
                           p["g_ffn"], p["w_query"])
    n1, e1, c2, e2 = _peer_scores(qp, p["keys_pad"])
    return _peer_mix(xnt, p["e_down"], p["e_up"], n1, e1, c2, e2, x1)


def _trunk(x, layers):
    batch, seq, _ = x.shape
    x2 = x.reshape(batch * seq, D_MODEL)
    cos_t, sin_t = _rope_tables(seq)
    for p in layers:
        x2 = _layer(x2, p, batch, seq, cos_t, sin_t)
    return x2.reshape(batch, seq, D_MODEL)


def _layer_params(l, g_mix, w_in, g_q, g_k, conv_w, conv_b, w_gate_a, b_gate_a, w_gate_x, b_gate_x, lru_lambda,
                  g_attn_out, g_lru_out, w_out, g_ffn, w_query, sub_keys, expert_down, expert_up):
    w_in_b = w_in[l].astype(BF16)
    w_out_b = w_out[l].astype(BF16)
    return {
        "g_mix": g_mix[l][None, :],
        "w_qkv": w_in_b[:, :QKV_WIDTH],
        "w_xy": w_in_b[:, QKV_WIDTH:],
        "g_q": g_q[l][None, :],
        "g_k": g_k[l][None, :],
        "conv_w": conv_w[l],
        "conv_b": conv_b[l][None, :],
        "w_gates": [_gate_weights(w_gate_a[l, d], w_gate_x[l, d]) for d in range(2)],
        "b_gates": [jnp.stack([b_gate_a[l, d], b_gate_x[l, d]], axis=0) for d in range(2)],
        "lam": [lru_lambda[l, d][None, :] for d in range(2)],
        "g_attn_out": g_attn_out[l][None, :],
        "g_lru_out": g_lru_out[l][None, :],
        "w_out_a": w_out_b[:ATTN_WIDTH],
        "w_out_l": w_out_b[ATTN_WIDTH:],
        "g_ffn": g_ffn[l][None, :],
        "w_query": w_query[l].astype(BF16),
        "keys_pad": _padded_keys(sub_keys[l]),
        "e_down": expert_down[l].astype(BF16),
        "e_up": expert_up[l].astype(BF16),
    }


def kernel(x_prompt, x_sample, g_mix, w_in, g_q, g_k, conv_w, conv_b, w_gate_a, b_gate_a, w_gate_x, b_gate_x, lru_lambda, g_attn_out, g_lru_out, w_out, g_ffn, w_query, sub_keys, expert_down, expert_up):
    weights = (g_mix, w_in, g_q, g_k, conv_w, conv_b, w_gate_a, b_gate_a, w_gate_x, b_gate_x, lru_lambda,
               g_attn_out, g_lru_out, w_out, g_ffn, w_query, sub_keys, expert_down, expert_up)
    layers = [_layer_params(l, *weights) for l in range(w_in.shape[0])]
    return (_trunk(x_prompt, layers), _trunk(x_sample, layers))
```

```python
import functools
import math

import jax
import jax.numpy as jnp
from jax import lax
from jax.experimental import pallas as pl
from jax.experimental.pallas import tpu as pltpu

F32 = jnp.float32
BF16 = jnp.bfloat16

D_MODEL = 2048
GRID_W = 64
N_HEADS = 8
N_KV_HEADS = 2
HEAD_DIM = 128
HEADS_PER_KV = N_HEADS // N_KV_HEADS
ATTN_WIDTH = N_HEADS * HEAD_DIM
KV_WIDTH = N_KV_HEADS * HEAD_DIM
QKV_WIDTH = ATTN_WIDTH + 2 * KV_WIDTH
ROPE_THETA = 10000.0
LRU_WIDTH = D_MODEL - ATTN_WIDTH
LRU_BLOCK_W = 64
CONV_W = 4
CONV_PAD_LEFT = 2
LRU_C = 8.0
N_KEYS = 128
N_EXPERTS = N_KEYS * N_KEYS
PEER_HEADS = 8
PEER_KEY_DIM = 128
PEER_HALF = PEER_KEY_DIM // 2
PEER_TOPK = 16
EPS = 1e-6

V7X_SUBLANES = 8
V7X_LANES = 128
V7X_MXU_DIM = 256
V7X_VMEM_LIMIT_BYTES = 56 * 1024 * 1024

NEG_INF = float("-inf")


def _params(*semantics, flags=None):
    return pltpu.CompilerParams(dimension_semantics=semantics,
                                vmem_limit_bytes=V7X_VMEM_LIMIT_BYTES,
                                flags=flags)


def _block(n, target):
    b = min(n, target)
    while n % b:
        b //= 2
    return b


def _gelu(x):
    return 0.5 * x * (1.0 + lax.erf(x * (1.0 / math.sqrt(2.0))))


def _sigmoid(x):
    return 0.5 + 0.5 * jnp.tanh(0.5 * x)


def _rms(x, g):
    return x * lax.rsqrt(jnp.mean(x * x, axis=-1, keepdims=True) + EPS) * g


def _qkv_kernel(x_ref, gmix_ref, w_ref, gq_ref, gk_ref, cos_ref, sin_ref,
                q_ref, k_ref, v_ref):
    h = _rms(x_ref[...], gmix_ref[...]).astype(BF16)
    z = jnp.dot(h, w_ref[...], preferred_element_type=F32)
    cos = cos_ref[...]
    sin = sin_ref[...]
    lane = lax.broadcasted_iota(jnp.int32, cos.shape, 1)
    first_half = (lane % (HEAD_DIM // 2)) < (HEAD_DIM // 4)
    scale = HEAD_DIM ** -0.5 * math.log2(math.e)

    def norm_rope(zh, g):
        y = _rms(zh, g)
        partner = jnp.where(first_half,
                            pltpu.roll(y, HEAD_DIM - HEAD_DIM // 4, 1),
                            pltpu.roll(y, HEAD_DIM // 4, 1))
        return y * cos + partner * sin

    gq = gq_ref[...]
    gk = gk_ref[...]
    for hd in range(N_HEADS):
        sl = slice(hd * HEAD_DIM, (hd + 1) * HEAD_DIM)
        q_ref[:, sl] = (norm_rope(z[:, sl], gq) * scale).astype(BF16)
    for hd in range(N_KV_HEADS):
        sl = slice(hd * HEAD_DIM, (hd + 1) * HEAD_DIM)
        k_ref[:, sl] = norm_rope(z[:, ATTN_WIDTH + hd * HEAD_DIM:ATTN_WIDTH + (hd + 1) * HEAD_DIM], gk).astype(BF16)
    ones = jnp.ones((z.shape[0], HEAD_DIM), BF16)
    for hd in range(N_KV_HEADS):
        v0 = ATTN_WIDTH + KV_WIDTH + hd * HEAD_DIM
        v_ref[:, 2 * hd * HEAD_DIM:(2 * hd + 1) * HEAD_DIM] = z[:, v0:v0 + HEAD_DIM].astype(BF16)
        v_ref[:, (2 * hd + 1) * HEAD_DIM:(2 * hd + 2) * HEAD_DIM] = ones


def _qkv_proj(x2, g_mix, w_qkv, g_q, g_k, cos_t, sin_t, seq):
    n = x2.shape[0]
    tm = _block(seq, 512)
    nseq_blocks = seq // tm
    row = lambda i: (i, 0)
    fixed = lambda i: (0, 0)
    pos = lambda i: (i % nseq_blocks, 0)
    return pl.pallas_call(
        _qkv_kernel,
        grid=(n // tm,),
        in_specs=[
            pl.BlockSpec((tm, D_MODEL), row),
            pl.BlockSpec((1, D_MODEL), fixed),
            pl.BlockSpec((D_MODEL, QKV_WIDTH), fixed),
            pl.BlockSpec((1, HEAD_DIM), fixed),
            pl.BlockSpec((1, HEAD_DIM), fixed),
            pl.BlockSpec((tm, HEAD_DIM), pos),
            pl.BlockSpec((tm, HEAD_DIM), pos),
        ],
        out_specs=[
            pl.BlockSpec((tm, ATTN_WIDTH), row),
            pl.BlockSpec((tm, KV_WIDTH), row),
            pl.BlockSpec((tm, 2 * KV_WIDTH), row),
        ],
        out_shape=[
            jax.ShapeDtypeStruct((n, ATTN_WIDTH), BF16),
            jax.ShapeDtypeStruct((n, KV_WIDTH), BF16),
            jax.ShapeDtypeStruct((n, 2 * KV_WIDTH), BF16),
        ],
        compiler_params=_params("parallel"),
        name="qkv_proj",
    )(x2, g_mix, w_qkv, g_q, g_k, cos_t, sin_t)


def _xy_kernel(x_ref, gmix_ref, w_ref, xr_ref, yr_ref):
    h = _rms(x_ref[...], gmix_ref[...]).astype(BF16)
    z = jnp.dot(h, w_ref[...], preferred_element_type=F32)
    xr_ref[...] = z[:, :LRU_WIDTH]
    yr_ref[...] = z[:, LRU_WIDTH:]


def _xy_proj(x2, g_mix, w_xy):
    n = x2.shape[0]
    tm = _block(n, 512)
    row = lambda i: (i, 0)
    fixed = lambda i: (0, 0)
    return pl.pallas_call(
        _xy_kernel,
        grid=(n // tm,),
        in_specs=[
            pl.BlockSpec((tm, D_MODEL), row),
            pl.BlockSpec((1, D_MODEL), fixed),
            pl.BlockSpec((D_MODEL, 2 * LRU_WIDTH), fixed),
        ],
        out_specs=[pl.BlockSpec((tm, LRU_WIDTH), row), pl.BlockSpec((tm, LRU_WIDTH), row)],
        out_shape=[jax.ShapeDtypeStruct((n, LRU_WIDTH), F32)] * 2,
        compiler_params=_params("parallel"),
        name="xy_proj",
    )(x2, g_mix, w_xy)


def _attn_kernel(q_ref, k_ref, v_ref, o_ref, m_scr, acc_scr):
    ki = pl.program_id(3)

    @pl.when(ki == 0)
    def _():
        m_scr[...] = jnp.full(m_scr.shape, NEG_INF, F32)
        acc_scr[...] = jnp.zeros(acc_scr.shape, F32)

    k = k_ref[...]
    v = v_ref[...]
    tk = k.shape[0]
    tq = q_ref.shape[0]
    lane_tiles = tk // HEAD_DIM
    nt = (((1,), (1,)), ((), ()))
    for hd in range(HEADS_PER_KV):
        q = q_ref[:, hd * HEAD_DIM:(hd + 1) * HEAD_DIM]
        if hd == 0 and tk % (2 * V7X_MXU_DIM) == 0:
            s = jnp.concatenate([lax.dot_general(q, k[:tk // 2], nt, preferred_element_type=F32),
                                 lax.dot_general(q, k[tk // 2:], nt, preferred_element_type=F32)], axis=1)
        else:
            s = lax.dot_general(q, k, nt, preferred_element_type=F32)
        m_prev = m_scr[hd]
        m_next = jnp.maximum(m_prev, jnp.max(s, axis=1, keepdims=True))
        p = jnp.exp2(s - jnp.tile(m_next, (1, lane_tiles))).astype(BF16)
        alpha = jnp.exp2(m_prev - m_next)
        if hd == HEADS_PER_KV - 1 and tq % 32 == 0:
            pv = jnp.concatenate([jnp.dot(p[:tq // 2], v, preferred_element_type=F32),
                                  jnp.dot(p[tq // 2:], v, preferred_element_type=F32)], axis=0)
        else:
            pv = jnp.dot(p, v, preferred_element_type=F32)
        acc_scr[hd] = jnp.tile(alpha, (1, 2)) * acc_scr[hd] + pv
        m_scr[hd] = m_next

    @pl.when(ki == pl.num_programs(3) - 1)
    def _():
        for hd in range(HEADS_PER_KV):
            acc = acc_scr[hd]
            o_ref[:, hd * HEAD_DIM:(hd + 1) * HEAD_DIM] = acc[:, :HEAD_DIM] / acc[:, HEAD_DIM:]


def _attention(q, k, v, batch, seq):
    n = q.shape[0]
    tq = _block(seq, 1024)
    tk = _block(seq, 2048)
    nq, nk = seq // tq, seq // tk
    group_w = HEADS_PER_KV * HEAD_DIM
    return pl.pallas_call(
        _attn_kernel,
        grid=(batch, N_KV_HEADS, nq, nk),
        in_specs=[
            pl.BlockSpec((tq, group_w), lambda b, g, qi, ki: (b * nq + qi, g)),
            pl.BlockSpec((tk, HEAD_DIM), lambda b, g, qi, ki: (b * nk + ki, g)),
            pl.BlockSpec((tk, 2 * HEAD_DIM), lambda b, g, qi, ki: (b * nk + ki, g)),
        ],
        out_specs=pl.BlockSpec((tq, group_w), lambda b, g, qi, ki: (b * nq + qi, g)),
        out_shape=jax.ShapeDtypeStruct((n, ATTN_WIDTH), F32),
        scratch_shapes=[
            pltpu.VMEM((HEADS_PER_KV, tq, HEAD_DIM), F32),
            pltpu.VMEM((HEADS_PER_KV, tq, 2 * HEAD_DIM), F32),
        ],
        compiler_params=_params("parallel", "parallel", "parallel", "arbitrary"),
        name="flash_attention",
    )(q, k, v)


def _lru_kernel(reverse, tc, n_chunks, *refs):
    if reverse:
        (xm_ref, xp_ref, xn_ref, cw_ref, cb_ref, wg_ref, bg_ref, lam_ref, hf_ref, yr_ref,
         out_ref, ext_scr, a_scr, u_scr, carry_scr) = refs
    else:
        (xm_ref, xp_ref, xn_ref, cw_ref, cb_ref, wg_ref, bg_ref, lam_ref,
         out_ref, ext_scr, a_scr, u_scr, carry_scr) = refs
    c = pl.program_id(1)
    chunk = (n_chunks - 1 - c) if reverse else c
    halo = V7X_SUBLANES

    @pl.when(c == 0)
    def _():
        carry_scr[...] = jnp.zeros(carry_scr.shape, F32)

    ext_scr[0:halo, :] = jnp.where(chunk == 0, 0.0, xp_ref[...])
    ext_scr[halo:halo + tc, :] = xm_ref[...]
    ext_scr[halo + tc:2 * halo + tc, :] = jnp.where(chunk == n_chunks - 1, 0.0, xn_ref[...])
    xc = cb_ref[...]
    for tap in range(CONV_W):
        start = halo + tap - CONV_PAD_LEFT
        xc = xc + ext_scr[start:start + tc, :] * cw_ref[tap:tap + 1, :]

    xcb = xc.astype(BF16)
    za, zx = [], []
    for g in range(LRU_WIDTH // V7X_MXU_DIM):
        zg = jnp.dot(xcb[:, g * V7X_MXU_DIM:(g + 1) * V7X_MXU_DIM], wg_ref[g],
                     preferred_element_type=F32)
        za.append(zg[:, :V7X_MXU_DIM])
        zx.append(zg[:, V7X_MXU_DIM:])
    r = _sigmoid(jnp.concatenate(za, axis=1) + bg_ref[0:1, :])
    i = _sigmoid(jnp.concatenate(zx, axis=1) + bg_ref[1:2, :])
    lam = lam_ref[...]
    softplus_neg = jnp.maximum(-lam, 0.0) + jnp.log(1.0 + jnp.exp(-jnp.abs(lam)))
    a = jnp.exp((-LRU_C * softplus_neg) * r)
    u = jnp.sqrt(1.0 - a * a) * (i * xc)

    nt = tc // V7X_SUBLANES
    a3 = a.reshape(nt, V7X_SUBLANES, LRU_WIDTH)
    u3 = u.reshape(nt, V7X_SUBLANES, LRU_WIDTH)
    row = lax.broadcasted_iota(jnp.int32, a3.shape, 1)
    for s in (1, 2, 4):
        if reverse:
            shift, valid = V7X_SUBLANES - s, row < V7X_SUBLANES - s
        else:
            shift, valid = s, row >= s
        a_sh = pltpu.roll(a3, shift, 1)
        u_sh = pltpu.roll(u3, shift, 1)
        u3 = u3 + jnp.where(valid, a3 * u_sh, 0.0)
        a3 = jnp.where(valid, a3 * a_sh, a3)
    a_scr[...] = a3
    u_scr[...] = u3

    def tile_step(t, carry):
        tt = (nt - 1 - t) if reverse else t
        h = u_scr[tt] + a_scr[tt] * carry
        u_scr[tt] = h
        edge = h[0:1, :] if reverse else h[V7X_SUBLANES - 1:V7X_SUBLANES, :]
        return jnp.broadcast_to(edge, h.shape)

    carry_scr[...] = lax.fori_loop(0, nt, tile_step, carry_scr[...])
    h_all = u_scr[...].reshape(tc, LRU_WIDTH)
    if reverse:
        out_ref[...] = (h_all + hf_ref[...]) * _gelu(yr_ref[...])
    else:
        out_ref[...] = h_all


def _lru_direction(reverse, xr, conv_w, conv_b, w_gates, b_gates, lam, batch, seq, h_fwd=None, yr=None):
    n = xr.shape[0]
    tc = _block(seq, 256)
    n_chunks = seq // tc
    halo = V7X_SUBLANES
    per_tile = tc // halo
    n_tiles = n // halo

    def chunk_of(c):
        return (n_chunks - 1 - c) if reverse else c

    main = lambda b, c: (b * n_chunks + chunk_of(c), 0)
    prev = lambda b, c: (jnp.maximum((b * n_chunks + chunk_of(c)) * per_tile - 1, 0), 0)
    nxt = lambda b, c: (jnp.minimum((b * n_chunks + chunk_of(c) + 1) * per_tile, n_tiles - 1), 0)
    fixed2 = lambda b, c: (0, 0)
    fixed3 = lambda b, c: (0, 0, 0)
    in_specs = [
        pl.BlockSpec((tc, LRU_WIDTH), main),
        pl.BlockSpec((halo, LRU_WIDTH), prev),
        pl.BlockSpec((halo, LRU_WIDTH), nxt),
        pl.BlockSpec((CONV_W, LRU_WIDTH), fixed2),
        pl.BlockSpec((1, LRU_WIDTH), fixed2),
        pl.BlockSpec((LRU_WIDTH // V7X_MXU_DIM, V7X_MXU_DIM, 2 * V7X_MXU_DIM), fixed3),
        pl.BlockSpec((2, LRU_WIDTH), fixed2),
        pl.BlockSpec((1, LRU_WIDTH), fixed2),
    ]
    args = [xr, xr, xr, conv_w, conv_b, w_gates, b_gates, lam]
    if reverse:
        in_specs += [pl.BlockSpec((tc, LRU_WIDTH), main), pl.BlockSpec((tc, LRU_WIDTH), main)]
        args += [h_fwd, yr]
    return pl.pallas_call(
        functools.partial(_lru_kernel, reverse, tc, n_chunks),
        grid=(batch, n_chunks),
        in_specs=in_specs,
        out_specs=pl.BlockSpec((tc, LRU_WIDTH), main),
        out_shape=jax.ShapeDtypeStruct((n, LRU_WIDTH), F32),
        scratch_shapes=[
            pltpu.VMEM((tc + 2 * halo, LRU_WIDTH), F32),
            pltpu.VMEM((per_tile, halo, LRU_WIDTH), F32),
            pltpu.VMEM((per_tile, halo, LRU_WIDTH), F32),
            pltpu.VMEM((halo, LRU_WIDTH), F32),
        ],
        compiler_params=_params("parallel", "arbitrary"),
        name="lru_bwd" if reverse else "lru_fwd",
    )(*args)


def _out_kernel(x_ref, attn_ref, lru_ref, ga_ref, gl_ref, wa_ref, wl_ref, gf_ref, wq_ref,
                x1_ref, xnt_ref, qp_ref):
    an = _rms(attn_ref[...], ga_ref[...]).astype(BF16)
    ln = _rms(lru_ref[...], gl_ref[...]).astype(BF16)
    x1 = (x_ref[...] + jnp.dot(an, wa_ref[...], preferred_element_type=F32)
          + jnp.dot(ln, wl_ref[...], preferred_element_type=F32))
    x1_ref[...] = x1
    xn = _rms(x1, gf_ref[...])
    xnt_ref[...] = xn.T.astype(BF16)
    qp_ref[...] = jnp.dot(xn.astype(BF16), wq_ref[...], preferred_element_type=F32)


def _out_proj(x2, attn, lru, g_attn, g_lru, w_out_a, w_out_l, g_ffn, w_query):
    n = x2.shape[0]
    tm = _block(n, 512)
    qw = PEER_HEADS * PEER_KEY_DIM
    row = lambda i: (i, 0)
    fixed = lambda i: (0, 0)
    return pl.pallas_call(
        _out_kernel,
        grid=(n // tm,),
        in_specs=[
            pl.BlockSpec((tm, D_MODEL), row),
            pl.BlockSpec((tm, ATTN_WIDTH), row),
            pl.BlockSpec((tm, LRU_WIDTH), row),
            pl.BlockSpec((1, ATTN_WIDTH), fixed),
            pl.BlockSpec((1, LRU_WIDTH), fixed),
            pl.BlockSpec((ATTN_WIDTH, D_MODEL), fixed),
            pl.BlockSpec((LRU_WIDTH, D_MODEL), fixed),
            pl.BlockSpec((1, D_MODEL), fixed),
            pl.BlockSpec((D_MODEL, qw), fixed),
        ],
        out_specs=[
            pl.BlockSpec((tm, D_MODEL), row),
            pl.BlockSpec((D_MODEL, tm), lambda i: (0, i)),
            pl.BlockSpec((tm, qw), row),
        ],
        out_shape=[
            jax.ShapeDtypeStruct((n, D_MODEL), F32),
            jax.ShapeDtypeStruct((D_MODEL, n), BF16),
            jax.ShapeDtypeStruct((n, qw), F32),
        ],
        compiler_params=_params("parallel"),
        name="out_proj",
    )(x2, attn, lru, g_attn, g_lru, w_out_a, w_out_l, g_ffn, w_query)


def _top_values(s, count, with_rank=False):
    vals = []
    rank = jnp.full(s.shape, float(count), F32) if with_rank else None
    for r in range(count):
        m = jnp.max(s, axis=0, keepdims=True)
        vals.append(m)
        hit = s == m
        if with_rank:
            rank = jnp.minimum(rank, jnp.where(hit, float(r), float(count)))
        s = jnp.where(hit, NEG_INF, s)
    return (vals, rank) if with_rank else vals


def _peer_score_kernel(qp_ref, keys_ref, n1_ref, e1_ref, c2_ref, e2_ref):
    half_tile = V7X_SUBLANES
    sub = lax.broadcasted_iota(jnp.int32, (half_tile, qp_ref.shape[0]), 0)
    for hd in range(PEER_HEADS):
        qh = qp_ref[:, hd * PEER_KEY_DIM:(hd + 1) * PEER_KEY_DIM].astype(BF16)
        nt = (((1,), (1,)), ((), ()))
        s1 = lax.dot_general(keys_ref[hd, 0], qh, nt, preferred_element_type=F32)
        s2 = lax.dot_general(keys_ref[hd, 1], qh, nt, preferred_element_type=F32)
        top1 = _top_values(s1, PEER_TOPK)
        top2_rows, rank2 = _top_values(s2, PEER_TOPK, with_rank=True)
        top2 = jnp.concatenate(top2_rows, axis=0)
        cands = [top1[0] + top2]
        for r in range(1, PEER_TOPK):
            ok = sub < PEER_TOPK // (r + 1)
            cands.append(jnp.where(ok, top1[r] + top2[:half_tile, :], NEG_INF))
        tau = _top_values(jnp.concatenate(cands, axis=0), PEER_TOPK)[-1]
        m12 = top1[0] + top2[0:1, :]
        z = jnp.zeros_like(tau)
        n1 = jnp.zeros_like(s1)
        for r in range(PEER_TOPK):
            picked = cands[r] >= tau
            z = z + jnp.sum(jnp.where(picked, jnp.exp(cands[r] - m12), 0.0), axis=0, keepdims=True)
            n_r = jnp.sum(jnp.where(picked, 1.0, 0.0), axis=0, keepdims=True)
            n1 = jnp.where(s1 == top1[r], n_r, n1)
        n1_ref[hd] = n1
        e1_ref[hd] = jnp.exp(s1 - top1[0]) / z
        c2_ref[hd] = rank2.astype(BF16)
        e2_ref[hd] = jnp.exp(s2 - top2[0:1, :]).astype(BF16)


def _peer_scores(qp, keys_pad):
    n = qp.shape[0]
    tb = _block(n, 256)
    qw = PEER_HEADS * PEER_KEY_DIM
    big = pl.BlockSpec((PEER_HEADS, N_KEYS, tb), lambda i: (0, 0, i))
    shape_f32 = jax.ShapeDtypeStruct((PEER_HEADS, N_KEYS, n), F32)
    shape_bf16 = jax.ShapeDtypeStruct((PEER_HEADS, N_KEYS, n), BF16)
    return pl.pallas_call(
        _peer_score_kernel,
        grid=(n // tb,),
        in_specs=[
            pl.BlockSpec((tb, qw), lambda i: (i, 0)),
            pl.BlockSpec((PEER_HEADS, 2, N_KEYS, PEER_KEY_DIM), lambda i: (0, 0, 0, 0)),
        ],
        out_specs=[big, big, big, big],
        out_shape=[shape_f32, shape_f32, shape_bf16, shape_bf16],
        compiler_params=_params("parallel"),
        name="peer_scores",
    )(qp, keys_pad)


PEER_KEYS_PER_STEP = V7X_SUBLANES
PEER_EXPERTS_PER_STEP = PEER_KEYS_PER_STEP * N_KEYS


def _peer_gate(n1_ref, e1_ref, c2_ref, e2_ref, g_scr):
    tb = g_scr.shape[1]
    bf16_rows = 2 * V7X_SUBLANES

    def key_row(ref, hd, il):
        row = jnp.broadcast_to(ref[hd, il:il + 1, :], (bf16_rows, tb)).astype(BF16)
        return jnp.tile(row, (N_KEYS // bf16_rows, 1))

    for il in range(PEER_KEYS_PER_STEP):
        gate = jnp.zeros((N_KEYS, tb), BF16)
        for hd in range(PEER_HEADS):
            n_row = key_row(n1_ref, hd, il)
            gate = gate + jnp.where(c2_ref[hd] < n_row, e2_ref[hd], 0.0) * key_row(e1_ref, hd, il)
        g_scr[il * N_KEYS:(il + 1) * N_KEYS, :] = gate


def _peer_mix_kernel(xnt_ref, ed_ref, eu_ref, n1_ref, e1_ref, c2_ref, e2_ref, x1_ref, y_ref, g_scr, w_scr):
    ec = pl.program_id(1)

    @pl.when(ec == 0)
    def _():
        y_ref[...] = x1_ref[...]

    _peer_gate(n1_ref, e1_ref, c2_ref, e2_ref, g_scr)
    hid = jnp.dot(ed_ref[...], xnt_ref[...], preferred_element_type=F32)
    w_scr[...] = _gelu(hid.astype(BF16)) * g_scr[...]
    tn = (((0,), (0,)), ((), ()))
    y_ref[...] += lax.dot_general(w_scr[...], eu_ref[...], tn, preferred_element_type=F32)


def _peer_mix(xnt, e_down, e_up, n1, e1, c2, e2, x1):
    n = xnt.shape[1]
    tb = _block(n, 512)
    ec = PEER_EXPERTS_PER_STEP
    tok = lambda t, e: (t, 0)
    exp_ = lambda t, e: (e, 0)
    key1 = pl.BlockSpec((PEER_HEADS, PEER_KEYS_PER_STEP, tb), lambda t, e: (0, e, t))
    key2 = pl.BlockSpec((PEER_HEADS, N_KEYS, tb), lambda t, e: (0, 0, t))
    return pl.pallas_call(
        _peer_mix_kernel,
        grid=(n // tb, N_EXPERTS // ec),
        in_specs=[
            pl.BlockSpec((D_MODEL, tb), lambda t, e: (0, t)),
            pl.BlockSpec((ec, D_MODEL), exp_),
            pl.BlockSpec((ec, D_MODEL), exp_),
            key1, key1, key2, key2,
            pl.BlockSpec((tb, D_MODEL), tok),
        ],
        out_specs=pl.BlockSpec((tb, D_MODEL), tok),
        out_shape=jax.ShapeDtypeStruct((n, D_MODEL), F32),
        scratch_shapes=[pltpu.VMEM((ec, tb), BF16), pltpu.VMEM((ec, tb), BF16)],
        compiler_params=_params("parallel", "arbitrary"),
        name="peer_mix",
    )(xnt, e_down, e_up, n1, e1, c2, e2, x1)


def _rope_tables(seq):
    t = jnp.arange(seq, dtype=jnp.int32)
    row = (t // GRID_W).astype(F32)
    col = (t % GRID_W).astype(F32)
    half = HEAD_DIM // 2
    inv = ROPE_THETA ** (-jnp.arange(0, half, 2, dtype=F32) / half)
    ar = row[:, None] * inv
    ac = col[:, None] * inv
    cos_t = jnp.concatenate([jnp.cos(ar), jnp.cos(ar), jnp.cos(ac), jnp.cos(ac)], axis=-1)
    sin_t = jnp.concatenate([-jnp.sin(ar), jnp.sin(ar), -jnp.sin(ac), jnp.sin(ac)], axis=-1)
    return cos_t, sin_t


def _gate_weights(w_gate_a, w_gate_x):
    per_tile = V7X_MXU_DIM // LRU_BLOCK_W
    n_tiles = LRU_WIDTH // V7X_MXU_DIM

    def tiles(w):
        w = w.reshape(n_tiles, per_tile, LRU_BLOCK_W, LRU_BLOCK_W)
        eye = jnp.eye(per_tile, dtype=w.dtype)
        return jnp.einsum("tpkj,pq->tpkqj", w, eye).reshape(n_tiles, V7X_MXU_DIM, V7X_MXU_DIM)

    return jnp.concatenate([tiles(w_gate_a), tiles(w_gate_x)], axis=-1).astype(BF16)


def _padded_keys(sub_keys):
    z = jnp.zeros_like(sub_keys[:, 0])
    k0 = jnp.concatenate([sub_keys[:, 0], z], axis=-1)
    k1 = jnp.concatenate([z, sub_keys[:, 1]], axis=-1)
    return jnp.stack([k0, k1], axis=1).astype(BF16)


def _layer(x2, p, batch, seq, cos_t, sin_t):
    q, k, v = _qkv_proj(x2, p["g_mix"], p["w_qkv"], p["g_q"], p["g_k"], cos_t, sin_t, seq)
    attn = _attention(q, k, v, batch, seq)
    xr, yr = _xy_proj(x2, p["g_mix"], p["w_xy"])
    h_fwd = _lru_direction(False, xr, p["conv_w"], p["conv_b"], p["w_gates"][0], p["b_gates"][0],
                           p["lam"][0], batch, seq)
    lru = _lru_direction(True, xr, p["conv_w"], p["conv_b"], p["w_gates"][1], p["b_gates"][1],
                         p["lam"][1], batch, seq, h_fwd=h_fwd, yr=yr)
    x1, xnt, qp = _out_proj(x2, attn, lru, p["g_attn_out"], p["g_lru_out"], p["w_out_a"], p["w_out_l"],
                           p["g_ffn"], p["w_query"])
    n1, e1, c2, e2 = _peer_scores(qp, p["keys_pad"])
    return _peer_mix(xnt, p["e_down"], p["e_up"], n1, e1, c2, e2, x1)


def _trunk(x, layers):
    batch, seq, _ = x.shape
    x2 = x.reshape(batch * seq, D_MODEL)
    cos_t, sin_t = _rope_tables(seq)
    for p in layers:
        x2 = _layer(x2, p, batch, seq, cos_t, sin_t)
    return x2.reshape(batch, seq, D_MODEL)


def _layer_params(l, g_mix, w_in, g_q, g_k, conv_w, conv_b, w_gate_a, b_gate_a, w_gate_x, b_gate_x, lru_lambda,
                  g_attn_out, g_lru_out, w_out, g_ffn, w_query, sub_keys, expert_down, expert_up):
    w_in_b = w_in[l].astype(BF16)
    w_out_b = w_out[l].astype(BF16)
    return {
        "g_mix": g_mix[l][None, :],
        "w_qkv": w_in_b[:, :QKV_WIDTH],
        "w_xy": w_in_b[:, QKV_WIDTH:],
        "g_q": g_q[l][None, :],
        "g_k": g_k[l][None, :],
        "conv_w": conv_w[l],
        "conv_b": conv_b[l][None, :],
        "w_gates": [_gate_weights(w_gate_a[l, d], w_gate_x[l, d]) for d in range(2)],
        "b_gates": [jnp.stack([b_gate_a[l, d], b_gate_x[l, d]], axis=0) for d in range(2)],
        "lam": [lru_lambda[l, d][None, :] for d in range(2)],
        "g_attn_out": g_attn_out[l][None, :],
        "g_lru_out": g_lru_out[l][None, :],
        "w_out_a": w_out_b[:ATTN_WIDTH],
        "w_out_l": w_out_b[ATTN_WIDTH:],
        "g_ffn": g_ffn[l][None, :],
        "w_query": w_query[l].astype(BF16),
        "keys_pad": _padded_keys(sub_keys[l]),
        "e_down": expert_down[l].astype(BF16),
        "e_up": expert_up[l].astype(BF16),
    }


def kernel(x_prompt, x_sample, g_mix, w_in, g_q, g_k, conv_w, conv_b, w_gate_a, b_gate_a, w_gate_x, b_gate_x, lru_lambda, g_attn_out, g_lru_out, w_out, g_ffn, w_query, sub_keys, expert_down, expert_up):
    weights = (g_mix, w_in, g_q, g_k, conv_w, conv_b, w_gate_a, b_gate_a, w_gate_x, b_gate_x, lru_lambda,
               g_attn_out, g_lru_out, w_out, g_ffn, w_query, sub_keys, expert_down, expert_up)
    layers = [_layer_params(l, *weights) for l in range(w_in.shape[0])]
    return (_trunk(x_prompt, layers), _trunk(x_sample, layers))
```

```python
import functools
import math

import jax
import jax.numpy as jnp
from jax import lax
from jax.experimental import pallas as pl
from jax.experimental.pallas import tpu as pltpu

F32 = jnp.float32
BF16 = jnp.bfloat16

D_MODEL = 2048
GRID_W = 64
N_HEADS = 8
N_KV_HEADS = 2
HEAD_DIM = 128
HEADS_PER_KV = N_HEADS // N_KV_HEADS
ATTN_WIDTH = N_HEADS * HEAD_DIM
KV_WIDTH = N_KV_HEADS * HEAD_DIM
QKV_WIDTH = ATTN_WIDTH + 2 * KV_WIDTH
ROPE_THETA = 10000.0
LRU_WIDTH = D_MODEL - ATTN_WIDTH
LRU_BLOCK_W = 64
CONV_W = 4
CONV_PAD_LEFT = 2
LRU_C = 8.0
N_KEYS = 128
N_EXPERTS = N_KEYS * N_KEYS
PEER_HEADS = 8
PEER_KEY_DIM = 128
PEER_HALF = PEER_KEY_DIM // 2
PEER_TOPK = 16
EPS = 1e-6

V7X_SUBLANES = 8
V7X_LANES = 128
V7X_MXU_DIM = 256
V7X_VMEM_LIMIT_BYTES = 56 * 1024 * 1024

NEG_INF = float("-inf")


def _params(*semantics, flags=None):
    return pltpu.CompilerParams(dimension_semantics=semantics,
                                vmem_limit_bytes=V7X_VMEM_LIMIT_BYTES,
                                flags=flags)


def _block(n, target):
    b = min(n, target)
    while n % b:
        b //= 2
    return b


def _gelu(x):
    return 0.5 * x * (1.0 + lax.erf(x * (1.0 / math.sqrt(2.0))))


def _sigmoid(x):
    return 0.5 + 0.5 * jnp.tanh(0.5 * x)


def _rms(x, g):
    return x * lax.rsqrt(jnp.mean(x * x, axis=-1, keepdims=True) + EPS) * g


def _qkv_kernel(x_ref, gmix_ref, w_ref, gq_ref, gk_ref, cos_ref, sin_ref,
                q_ref, k_ref, v_ref):
    h = _rms(x_ref[...], gmix_ref[...]).astype(BF16)
    z = jnp.dot(h, w_ref[...], preferred_element_type=F32)
    cos = cos_ref[...]
    sin = sin_ref[...]
    lane = lax.broadcasted_iota(jnp.int32, cos.shape, 1)
    first_half = (lane % (HEAD_DIM // 2)) < (HEAD_DIM // 4)
    scale = HEAD_DIM ** -0.5 * math.log2(math.e)

    def norm_rope(zh, g):
        y = _rms(zh, g)
        partner = jnp.where(first_half,
                            pltpu.roll(y, HEAD_DIM - HEAD_DIM // 4, 1),
                            pltpu.roll(y, HEAD_DIM // 4, 1))
        return y * cos + partner * sin

    gq = gq_ref[...]
    gk = gk_ref[...]
    for hd in range(N_HEADS):
        sl = slice(hd * HEAD_DIM, (hd + 1) * HEAD_DIM)
        q_ref[:, sl] = (norm_rope(z[:, sl], gq) * scale).astype(BF16)
    for hd in range(N_KV_HEADS):
        sl = slice(hd * HEAD_DIM, (hd + 1) * HEAD_DIM)
        k_ref[:, sl] = norm_rope(z[:, ATTN_WIDTH + hd * HEAD_DIM:ATTN_WIDTH + (hd + 1) * HEAD_DIM], gk).astype(BF16)
    ones = jnp.ones((z.shape[0], HEAD_DIM), BF16)
    for hd in range(N_KV_HEADS):
        v0 = ATTN_WIDTH + KV_WIDTH + hd * HEAD_DIM
        v_ref[:, 2 * hd * HEAD_DIM:(2 * hd + 1) * HEAD_DIM] = z[:, v0:v0 + HEAD_DIM].astype(BF16)
        v_ref[:, (2 * hd + 1) * HEAD_DIM:(2 * hd + 2) * HEAD_DIM] = ones


def _qkv_proj(x2, g_mix, w_qkv, g_q, g_k, cos_t, sin_t, seq):
    n = x2.shape[0]
    tm = _block(seq, 512)
    nseq_blocks = seq // tm
    row = lambda i: (i, 0)
    fixed = lambda i: (0, 0)
    pos = lambda i: (i % nseq_blocks, 0)
    return pl.pallas_call(
        _qkv_kernel,
        grid=(n // tm,),
        in_specs=[
            pl.BlockSpec((tm, D_MODEL), row),
            pl.BlockSpec((1, D_MODEL), fixed),
            pl.BlockSpec((D_MODEL, QKV_WIDTH), fixed),
            pl.BlockSpec((1, HEAD_DIM), fixed),
            pl.BlockSpec((1, HEAD_DIM), fixed),
            pl.BlockSpec((tm, HEAD_DIM), pos),
            pl.BlockSpec((tm, HEAD_DIM), pos),
        ],
        out_specs=[
            pl.BlockSpec((tm, ATTN_WIDTH), row),
            pl.BlockSpec((tm, KV_WIDTH), row),
            pl.BlockSpec((tm, 2 * KV_WIDTH), row),
        ],
        out_shape=[
            jax.ShapeDtypeStruct((n, ATTN_WIDTH), BF16),
            jax.ShapeDtypeStruct((n, KV_WIDTH), BF16),
            jax.ShapeDtypeStruct((n, 2 * KV_WIDTH), BF16),
        ],
        compiler_params=_params("parallel"),
        name="qkv_proj",
    )(x2, g_mix, w_qkv, g_q, g_k, cos_t, sin_t)


def _xy_kernel(x_ref, gmix_ref, w_ref, xr_ref, yr_ref):
    h = _rms(x_ref[...], gmix_ref[...]).astype(BF16)
    z = jnp.dot(h, w_ref[...], preferred_element_type=F32)
    xr_ref[...] = z[:, :LRU_WIDTH]
    yr_ref[...] = z[:, LRU_WIDTH:]


def _xy_proj(x2, g_mix, w_xy):
    n = x2.shape[0]
    tm = _block(n, 512)
    row = lambda i: (i, 0)
    fixed = lambda i: (0, 0)
    return pl.pallas_call(
        _xy_kernel,
        grid=(n // tm,),
        in_specs=[
            pl.BlockSpec((tm, D_MODEL), row),
            pl.BlockSpec((1, D_MODEL), fixed),
            pl.BlockSpec((D_MODEL, 2 * LRU_WIDTH), fixed),
        ],
        out_specs=[pl.BlockSpec((tm, LRU_WIDTH), row), pl.BlockSpec((tm, LRU_WIDTH), row)],
        out_shape=[jax.ShapeDtypeStruct((n, LRU_WIDTH), F32)] * 2,
        compiler_params=_params("parallel"),
        name="xy_proj",
    )(x2, g_mix, w_xy)


def _attn_kernel(q_ref, k_ref, v_ref, o_ref, m_scr, acc_scr):
    ki = pl.program_id(3)

    @pl.when(ki == 0)
    def _():
        m_scr[...] = jnp.full(m_scr.shape, NEG_INF, F32)
        acc_scr[...] = jnp.zeros(acc_scr.shape, F32)

    k = k_ref[...]
    v = v_ref[...]
    tk = k.shape[0]
    tq = q_ref.shape[0]
    lane_tiles = tk // HEAD_DIM
    nt = (((1,), (1,)), ((), ()))
    for hd in range(HEADS_PER_KV):
        q = q_ref[:, hd * HEAD_DIM:(hd + 1) * HEAD_DIM]
        if hd == 0 and tk % (2 * V7X_MXU_DIM) == 0:
            s = jnp.concatenate([lax.dot_general(q, k[:tk // 2], nt, preferred_element_type=F32),
                                 lax.dot_general(q, k[tk // 2:], nt, preferred_element_type=F32)], axis=1)
        else:
            s = lax.dot_general(q, k, nt, preferred_element_type=F32)
        m_prev = m_scr[hd]
        m_next = jnp.maximum(m_prev, jnp.max(s, axis=1, keepdims=True))
        p = jnp.exp2(s - jnp.tile(m_next, (1, lane_tiles))).astype(BF16)
        alpha = jnp.exp2(m_prev - m_next)
        if hd == HEADS_PER_KV - 1 and tq % 32 == 0:
            pv = jnp.concatenate([jnp.dot(p[:tq // 2], v, preferred_element_type=F32),
                                  jnp.dot(p[tq // 2:], v, preferred_element_type=F32)], axis=0)
        else:
            pv = jnp.dot(p, v, preferred_element_type=F32)
        acc_scr[hd] = jnp.tile(alpha, (1, 2)) * acc_scr[hd] + pv
        m_scr[hd] = m_next

    @pl.when(ki == pl.num_programs(3) - 1)
    def _():
        for hd in range(HEADS_PER_KV):
            acc = acc_scr[hd]
            o_ref[:, hd * HEAD_DIM:(hd + 1) * HEAD_DIM] = acc[:, :HEAD_DIM] / acc[:, HEAD_DIM:]


def _attention(q, k, v, batch, seq):
    n = q.shape[0]
    tq = _block(seq, 1024)
    tk = _block(seq, 2048)
    nq, nk = seq // tq, seq // tk
    group_w = HEADS_PER_KV * HEAD_DIM
    return pl.pallas_call(
        _attn_kernel,
        grid=(batch, N_KV_HEADS, nq, nk),
        in_specs=[
            pl.BlockSpec((tq, group_w), lambda b, g, qi, ki: (b * nq + qi, g)),
            pl.BlockSpec((tk, HEAD_DIM), lambda b, g, qi, ki: (b * nk + ki, g)),
            pl.BlockSpec((tk, 2 * HEAD_DIM), lambda b, g, qi, ki: (b * nk + ki, g)),
        ],
        out_specs=pl.BlockSpec((tq, group_w), lambda b, g, qi, ki: (b * nq + qi, g)),
        out_shape=jax.ShapeDtypeStruct((n, ATTN_WIDTH), F32),
        scratch_shapes=[
            pltpu.VMEM((HEADS_PER_KV, tq, HEAD_DIM), F32),
            pltpu.VMEM((HEADS_PER_KV, tq, 2 * HEAD_DIM), F32),
        ],
        compiler_params=_params("parallel", "parallel", "parallel", "arbitrary"),
        name="flash_attention",
    )(q, k, v)


def _lru_kernel(reverse, tc, n_chunks, *refs):
    if reverse:
        (xm_ref, xp_ref, xn_ref, cw_ref, cb_ref, wg_ref, bg_ref, lam_ref, hf_ref, yr_ref,
         out_ref, ext_scr, a_scr, u_scr, carry_scr) = refs
    else:
        (xm_ref, xp_ref, xn_ref, cw_ref, cb_ref, wg_ref, bg_ref, lam_ref,
         out_ref, ext_scr, a_scr, u_scr, carry_scr) = refs
    c = pl.program_id(1)
    chunk = (n_chunks - 1 - c) if reverse else c
    halo = V7X_SUBLANES

    @pl.when(c == 0)
    def _():
        carry_scr[...] = jnp.zeros(carry_scr.shape, F32)

    ext_scr[0:halo, :] = jnp.where(chunk == 0, 0.0, xp_ref[...])
    ext_scr[halo:halo + tc, :] = xm_ref[...]
    ext_scr[halo + tc:2 * halo + tc, :] = jnp.where(chunk == n_chunks - 1, 0.0, xn_ref[...])
    xc = cb_ref[...]
    for tap in range(CONV_W):
        start = halo + tap - CONV_PAD_LEFT
        xc = xc + ext_scr[start:start + tc, :] * cw_ref[tap:tap + 1, :]

    xcb = xc.astype(BF16)
    za, zx = [], []
    for g in range(LRU_WIDTH // V7X_MXU_DIM):
        zg = jnp.dot(xcb[:, g * V7X_MXU_DIM:(g + 1) * V7X_MXU_DIM], wg_ref[g],
                     preferred_element_type=F32)
        za.append(zg[:, :V7X_MXU_DIM])
        zx.append(zg[:, V7X_MXU_DIM:])
    r = _sigmoid(jnp.concatenate(za, axis=1) + bg_ref[0:1, :])
    i = _sigmoid(jnp.concatenate(zx, axis=1) + bg_ref[1:2, :])
    lam = lam_ref[...]
    softplus_neg = jnp.maximum(-lam, 0.0) + jnp.log(1.0 + jnp.exp(-jnp.abs(lam)))
    a = jnp.exp((-LRU_C * softplus_neg) * r)
    u = jnp.sqrt(1.0 - a * a) * (i * xc)

    nt = tc // V7X_SUBLANES
    a3 = a.reshape(nt, V7X_SUBLANES, LRU_WIDTH)
    u3 = u.reshape(nt, V7X_SUBLANES, LRU_WIDTH)
    row = lax.broadcasted_iota(jnp.int32, a3.shape, 1)
    for s in (1, 2, 4):
        if reverse:
            shift, valid = V7X_SUBLANES - s, row < V7X_SUBLANES - s
        else:
            shift, valid = s, row >= s
        a_sh = pltpu.roll(a3, shift, 1)
        u_sh = pltpu.roll(u3, shift, 1)
        u3 = u3 + jnp.where(valid, a3 * u_sh, 0.0)
        a3 = jnp.where(valid, a3 * a_sh, a3)
    a_scr[...] = a3
    u_scr[...] = u3

    def tile_step(t, carry):
        tt = (nt - 1 - t) if reverse else t
        h = u_scr[tt] + a_scr[tt] * carry
        u_scr[tt] = h
        edge = h[0:1, :] if reverse else h[V7X_SUBLANES - 1:V7X_SUBLANES, :]
        return jnp.broadcast_to(edge, h.shape)

    carry_scr[...] = lax.fori_loop(0, nt, tile_step, carry_scr[...])
    h_all = u_scr[...].reshape(tc, LRU_WIDTH)
    if reverse:
        out_ref[...] = (h_all + hf_ref[...]) * _gelu(yr_ref[...])
    else:
        out_ref[...] = h_all


def _lru_direction(reverse, xr, conv_w, conv_b, w_gates, b_gates, lam, batch, seq, h_fwd=None, yr=None):
    n = xr.shape[0]
    tc = _block(seq, 256)
    n_chunks = seq // tc
    halo = V7X_SUBLANES
    per_tile = tc // halo
    n_tiles = n // halo

    def chunk_of(c):
        return (n_chunks - 1 - c) if reverse else c

    main = lambda b, c: (b * n_chunks + chunk_of(c), 0)
    prev = lambda b, c: (jnp.maximum((b * n_chunks + chunk_of(c)) * per_tile - 1, 0), 0)
    nxt = lambda b, c: (jnp.minimum((b * n_chunks + chunk_of(c) + 1) * per_tile, n_tiles - 1), 0)
    fixed2 = lambda b, c: (0, 0)
    fixed3 = lambda b, c: (0, 0, 0)
    in_specs = [
        pl.BlockSpec((tc, LRU_WIDTH), main),
        pl.BlockSpec((halo, LRU_WIDTH), prev),
        pl.BlockSpec((halo, LRU_WIDTH), nxt),
        pl.BlockSpec((CONV_W, LRU_WIDTH), fixed2),
        pl.BlockSpec((1, LRU_WIDTH), fixed2),
        pl.BlockSpec((LRU_WIDTH // V7X_MXU_DIM, V7X_MXU_DIM, 2 * V7X_MXU_DIM), fixed3),
        pl.BlockSpec((2, LRU_WIDTH), fixed2),
        pl.BlockSpec((1, LRU_WIDTH), fixed2),
    ]
    args = [xr, xr, xr, conv_w, conv_b, w_gates, b_gates, lam]
    if reverse:
        in_specs += [pl.BlockSpec((tc, LRU_WIDTH), main), pl.BlockSpec((tc, LRU_WIDTH), main)]
        args += [h_fwd, yr]
    return pl.pallas_call(
        functools.partial(_lru_kernel, reverse, tc, n_chunks),
        grid=(batch, n_chunks),
        in_specs=in_specs,
        out_specs=pl.BlockSpec((tc, LRU_WIDTH), main),
        out_shape=jax.ShapeDtypeStruct((n, LRU_WIDTH), F32),
        scratch_shapes=[
            pltpu.VMEM((tc + 2 * halo, LRU_WIDTH), F32),
            pltpu.VMEM((per_tile, halo, LRU_WIDTH), F32),
            pltpu.VMEM((per_tile, halo, LRU_WIDTH), F32),
            pltpu.VMEM((halo, LRU_WIDTH), F32),
        ],
        compiler_params=_params("parallel", "arbitrary"),
        name="lru_bwd" if reverse else "lru_fwd",
    )(*args)


def _out_kernel(x_ref, attn_ref, lru_ref, ga_ref, gl_ref, wa_ref, wl_ref, gf_ref, wq_ref,
                x1_ref, xnt_ref, qp_ref):
    an = _rms(attn_ref[...], ga_ref[...]).astype(BF16)
    ln = _rms(lru_ref[...], gl_ref[...]).astype(BF16)
    x1 = (x_ref[...] + jnp.dot(an, wa_ref[...], preferred_element_type=F32)
          + jnp.dot(ln, wl_ref[...], preferred_element_type=F32))
    x1_ref[...] = x1
    xn = _rms(x1, gf_ref[...])
    xnt_ref[...] = xn.T.astype(BF16)
    qp_ref[...] = jnp.dot(xn.astype(BF16), wq_ref[...], preferred_element_type=F32)


def _out_proj(x2, attn, lru, g_attn, g_lru, w_out_a, w_out_l, g_ffn, w_query):
    n = x2.shape[0]
    tm = _block(n, 512)
    qw = PEER_HEADS * PEER_KEY_DIM
    row = lambda i: (i, 0)
    fixed = lambda i: (0, 0)
    return pl.pallas_call(
        _out_kernel,
        grid=(n // tm,),
        in_specs=[
            pl.BlockSpec((tm, D_MODEL), row),
            pl.BlockSpec((tm, ATTN_WIDTH), row),
            pl.BlockSpec((tm, LRU_WIDTH), row),
            pl.BlockSpec((1, ATTN_WIDTH), fixed),
            pl.BlockSpec((1, LRU_WIDTH), fixed),
            pl.BlockSpec((ATTN_WIDTH, D_MODEL), fixed),
            pl.BlockSpec((LRU_WIDTH, D_MODEL), fixed),
            pl.BlockSpec((1, D_MODEL), fixed),
            pl.BlockSpec((D_MODEL, qw), fixed),
        ],
        out_specs=[
            pl.BlockSpec((tm, D_MODEL), row),
            pl.BlockSpec((D_MODEL, tm), lambda i: (0, i)),
            pl.BlockSpec((tm, qw), row),
        ],
        out_shape=[
            jax.ShapeDtypeStruct((n, D_MODEL), F32),
            jax.ShapeDtypeStruct((D_MODEL, n), BF16),
            jax.ShapeDtypeStruct((n, qw), F32),
        ],
        compiler_params=_params("parallel"),
        name="out_proj",
    )(x2, attn, lru, g_attn, g_lru, w_out_a, w_out_l, g_ffn, w_query)


def _top_values(s, count, with_rank=False):
    vals = []
    rank = jnp.full(s.shape, float(count), F32) if with_rank else None
    for r in range(count):
        m = jnp.max(s, axis=0, keepdims=True)
        vals.append(m)
        hit = s == m
        if with_rank:
            rank = jnp.minimum(rank, jnp.where(hit, float(r), float(count)))
        s = jnp.where(hit, NEG_INF, s)
    return (vals, rank) if with_rank else vals


def _peer_score_kernel(qp_ref, keys_ref, n1_ref, e1_ref, c2_ref, e2_ref):
    half_tile = V7X_SUBLANES
    sub = lax.broadcasted_iota(jnp.int32, (half_tile, qp_ref.shape[0]), 0)
    for hd in range(PEER_HEADS):
        qh = qp_ref[:, hd * PEER_KEY_DIM:(hd + 1) * PEER_KEY_DIM].astype(BF16)
        nt = (((1,), (1,)), ((), ()))
        s1 = lax.dot_general(keys_ref[hd, 0], qh, nt, preferred_element_type=F32)
        s2 = lax.dot_general(keys_ref[hd, 1], qh, nt, preferred_element_type=F32)
        top1 = _top_values(s1, PEER_TOPK)
        top2_rows, rank2 = _top_values(s2, PEER_TOPK, with_rank=True)
        top2 = jnp.concatenate(top2_rows, axis=0)
        cands = [top1[0] + top2]
        for r in range(1, PEER_TOPK):
            ok = sub < PEER_TOPK // (r + 1)
            cands.append(jnp.where(ok, top1[r] + top2[:half_tile, :], NEG_INF))
        tau = _top_values(jnp.concatenate(cands, axis=0), PEER_TOPK)[-1]
        m12 = top1[0] + top2[0:1, :]
        z = jnp.zeros_like(tau)
        n1 = jnp.zeros_like(s1)
        for r in range(PEER_TOPK):
            picked = cands[r] >= tau
            z = z + jnp.sum(jnp.where(picked, jnp.exp(cands[r] - m12), 0.0), axis=0, keepdims=True)
            n_r = jnp.sum(jnp.where(picked, 1.0, 0.0), axis=0, keepdims=True)
            n1 = jnp.where(s1 == top1[r], n_r, n1)
        n1_ref[hd] = n1
        e1_ref[hd] = jnp.exp(s1 - top1[0]) / z
        c2_ref[hd] = rank2.astype(BF16)
        e2_ref[hd] = jnp.exp(s2 - top2[0:1, :]).astype(BF16)


def _peer_scores(qp, keys_pad):
    n = qp.shape[0]
    tb = _block(n, 256)
    qw = PEER_HEADS * PEER_KEY_DIM
    big = pl.BlockSpec((PEER_HEADS, N_KEYS, tb), lambda i: (0, 0, i))
    shape_f32 = jax.ShapeDtypeStruct((PEER_HEADS, N_KEYS, n), F32)
    shape_bf16 = jax.ShapeDtypeStruct((PEER_HEADS, N_KEYS, n), BF16)
    return pl.pallas_call(
        _peer_score_kernel,
        grid=(n // tb,),
        in_specs=[
            pl.BlockSpec((tb, qw), lambda i: (i, 0)),
            pl.BlockSpec((PEER_HEADS, 2, N_KEYS, PEER_KEY_DIM), lambda i: (0, 0, 0, 0)),
        ],
        out_specs=[big, big, big, big],
        out_shape=[shape_f32, shape_f32, shape_bf16, shape_bf16],
        compiler_params=_params("parallel"),
        name="peer_scores",
    )(qp, keys_pad)


PEER_KEYS_PER_STEP = V7X_SUBLANES
PEER_EXPERTS_PER_STEP = PEER_KEYS_PER_STEP * N_KEYS


def _peer_gate(n1_ref, e1_ref, c2_ref, e2_ref, g_scr):
    tb = g_scr.shape[1]
    bf16_rows = 2 * V7X_SUBLANES

    def key_row(ref, hd, il):
        row = jnp.broadcast_to(ref[hd, il:il + 1, :], (bf16_rows, tb)).astype(BF16)
        return jnp.tile(row, (N_KEYS // bf16_rows, 1))

    for il in range(PEER_KEYS_PER_STEP):
        gate = jnp.zeros((N_KEYS, tb), BF16)
        for hd in range(PEER_HEADS):
            n_row = key_row(n1_ref, hd, il)
            gate = gate + jnp.where(c2_ref[hd] < n_row, e2_ref[hd], 0.0) * key_row(e1_ref, hd, il)
        g_scr[il * N_KEYS:(il + 1) * N_KEYS, :] = gate


def _peer_mix_kernel(xnt_ref, ed_ref, eu_ref, n1_ref, e1_ref, c2_ref, e2_ref, x1_ref, y_ref, g_scr, w_scr):
    ec = pl.program_id(1)

    @pl.when(ec == 0)
    def _():
        y_ref[...] = x1_ref[...]

    _peer_gate(n1_ref, e1_ref, c2_ref, e2_ref, g_scr)
    tn = (((0,), (0,)), ((), ()))
    half = ed_ref.shape[0] // 2
    subs = [slice(0, half), slice(half, 2 * half)]
    hids = [jnp.dot(ed_ref[rows, :], xnt_ref[...], preferred_element_type=F32) for rows in subs]
    parts = []
    for c, rows in enumerate(subs):
        w_scr[rows, :] = _gelu(hids[c].astype(BF16)) * g_scr[rows, :]
        parts.append(lax.dot_general(w_scr[rows, :], eu_ref[rows, :], tn, preferred_element_type=F32))
    y_ref[...] += parts[0] + parts[1]


def _peer_mix(xnt, e_down, e_up, n1, e1, c2, e2, x1):
    n = xnt.shape[1]
    tb = _block(n, 512)
    ec = PEER_EXPERTS_PER_STEP
    tok = lambda t, e: (t, 0)
    exp_ = lambda t, e: (e, 0)
    key1 = pl.BlockSpec((PEER_HEADS, PEER_KEYS_PER_STEP, tb), lambda t, e: (0, e, t))
    key2 = pl.BlockSpec((PEER_HEADS, N_KEYS, tb), lambda t, e: (0, 0, t))
    return pl.pallas_call(
        _peer_mix_kernel,
        grid=(n // tb, N_EXPERTS // ec),
        in_specs=[
            pl.BlockSpec((D_MODEL, tb), lambda t, e: (0, t)),
            pl.BlockSpec((ec, D_MODEL), exp_),
            pl.BlockSpec((ec, D_MODEL), exp_),
            key1, key1, key2, key2,
            pl.BlockSpec((tb, D_MODEL), tok),
        ],
        out_specs=pl.BlockSpec((tb, D_MODEL), tok),
        out_shape=jax.ShapeDtypeStruct((n, D_MODEL), F32),
        scratch_shapes=[pltpu.VMEM((ec, tb), BF16), pltpu.VMEM((ec, tb), BF16)],
        compiler_params=_params("parallel", "arbitrary"),
        name="peer_mix",
    )(xnt, e_down, e_up, n1, e1, c2, e2, x1)


def _rope_tables(seq):
    t = jnp.arange(seq, dtype=jnp.int32)
    row = (t // GRID_W).astype(F32)
    col = (t % GRID_W).astype(F32)
    half = HEAD_DIM // 2
    inv = ROPE_THETA ** (-jnp.arange(0, half, 2, dtype=F32) / half)
    ar = row[:, None] * inv
    ac = col[:, None] * inv
    cos_t = jnp.concatenate([jnp.cos(ar), jnp.cos(ar), jnp.cos(ac), jnp.cos(ac)], axis=-1)
    sin_t = jnp.concatenate([-jnp.sin(ar), jnp.sin(ar), -jnp.sin(ac), jnp.sin(ac)], axis=-1)
    return cos_t, sin_t


def _gate_weights(w_gate_a, w_gate_x):
    per_tile = V7X_MXU_DIM // LRU_BLOCK_W
    n_tiles = LRU_WIDTH // V7X_MXU_DIM

    def tiles(w):
        w = w.reshape(n_tiles, per_tile, LRU_BLOCK_W, LRU_BLOCK_W)
        eye = jnp.eye(per_tile, dtype=w.dtype)
        return jnp.einsum("tpkj,pq->tpkqj", w, eye).reshape(n_tiles, V7X_MXU_DIM, V7X_MXU_DIM)

    return jnp.concatenate([tiles(w_gate_a), tiles(w_gate_x)], axis=-1).astype(BF16)


def _padded_keys(sub_keys):
    z = jnp.zeros_like(sub_keys[:, 0])
    k0 = jnp.concatenate([sub_keys[:, 0], z], axis=-1)
    k1 = jnp.concatenate([z, sub_keys[:, 1]], axis=-1)
    return jnp.stack([k0, k1], axis=1).astype(BF16)


def _layer(x2, p, batch, seq, cos_t, sin_t):
    q, k, v = _qkv_proj(x2, p["g_mix"], p["w_qkv"], p["g_q"], p["g_k"], cos_t, sin_t, seq)
    attn = _attention(q, k, v, batch, seq)
    xr, yr = _xy_proj(x2, p["g_mix"], p["w_xy"])
    h_fwd = _lru_direction(False, xr, p["conv_w"], p["conv_b"], p["w_gates"][0], p["b_gates"][0],
                           p["lam"][0], batch, seq)
    lru = _lru_direction(True, xr, p["conv_w"], p["conv_b"], p["w_gates"][1], p["b_gates"][1],
                         p["lam"][1], batch, seq, h_fwd=h_fwd, yr=yr)
    x1, xnt, qp = _out_proj(x2, attn, lru, p["g_attn_out"], p["g_lru_out"], p["w_out_a"], p["w_out_l"],
                           p["g_ffn"], p["w_query"])
    n1, e1, c2, e2 = _peer_scores(qp, p["keys_pad"])
    return _peer_mix(xnt, p["e_down"], p["e_up"], n1, e1, c2, e2, x1)


def _trunk(x, layers):
    batch, seq, _ = x.shape
    x2 = x.reshape(batch * seq, D_MODEL)
    cos_t, sin_t = _rope_tables(seq)
    for p in layers:
        x2 = _layer(x2, p, batch, seq, cos_t, sin_t)
    return x2.reshape(batch, seq, D_MODEL)


def _layer_params(l, g_mix, w_in, g_q, g_k, conv_w, conv_b, w_gate_a, b_gate_a, w_gate_x, b_gate_x, lru_lambda,
                  g_attn_out, g_lru_out, w_out, g_ffn, w_query, sub_keys, expert_down, expert_up):
    w_in_b = w_in[l].astype(BF16)
    w_out_b = w_out[l].astype(BF16)
    return {
        "g_mix": g_mix[l][None, :],
        "w_qkv": w_in_b[:, :QKV_WIDTH],
        "w_xy": w_in_b[:, QKV_WIDTH:],
        "g_q": g_q[l][None, :],
        "g_k": g_k[l][None, :],
        "conv_w": conv_w[l],
        "conv_b": conv_b[l][None, :],
        "w_gates": [_gate_weights(w_gate_a[l, d], w_gate_x[l, d]) for d in range(2)],
        "b_gates": [jnp.stack([b_gate_a[l, d], b_gate_x[l, d]], axis=0) for d in range(2)],
        "lam": [lru_lambda[l, d][None, :] for d in range(2)],
        "g_attn_out": g_attn_out[l][None, :],
        "g_lru_out": g_lru_out[l][None, :],
        "w_out_a": w_out_b[:ATTN_WIDTH],
        "w_out_l": w_out_b[ATTN_WIDTH:],
        "g_ffn": g_ffn[l][None, :],
        "w_query": w_query[l].astype(BF16),
        "keys_pad": _padded_keys(sub_keys[l]),
        "e_down": expert_down[l].astype(BF16),
        "e_up": expert_up[l].astype(BF16),
    }


def kernel(x_prompt, x_sample, g_mix, w_in, g_q, g_k, conv_w, conv_b, w_gate_a, b_gate_a, w_gate_x, b_gate_x, lru_lambda, g_attn_out, g_lru_out, w_out, g_ffn, w_query, sub_keys, expert_down, expert_up):
    weights = (g_mix, w_in, g_q, g_k, conv_w, conv_b, w_gate_a, b_gate_a, w_gate_x, b_gate_x, lru_lambda,
               g_attn_out, g_lru_out, w_out, g_ffn, w_query, sub_keys, expert_down, expert_up)
    layers = [_layer_params(l, *weights) for l in range(w_in.shape[0])]
    return (_trunk(x_prompt, layers), _trunk(x_sample, layers))
```

```python
import functools
import math

import jax
import jax.numpy as jnp
from jax import lax
from jax.experimental import pallas as pl
from jax.experimental.pallas import tpu as pltpu

F32 = jnp.float32
BF16 = jnp.bfloat16

D_MODEL = 2048
GRID_W = 64
N_HEADS = 8
N_KV_HEADS = 2
HEAD_DIM = 128
HEADS_PER_KV = N_HEADS // N_KV_HEADS
ATTN_WIDTH = N_HEADS * HEAD_DIM
KV_WIDTH = N_KV_HEADS * HEAD_DIM
QKV_WIDTH = ATTN_WIDTH + 2 * KV_WIDTH
ROPE_THETA = 10000.0
LRU_WIDTH = D_MODEL - ATTN_WIDTH
LRU_BLOCK_W = 64
CONV_W = 4
CONV_PAD_LEFT = 2
LRU_C = 8.0
N_KEYS = 128
N_EXPERTS = N_KEYS * N_KEYS
PEER_HEADS = 8
PEER_KEY_DIM = 128
PEER_HALF = PEER_KEY_DIM // 2
PEER_TOPK = 16
EPS = 1e-6

V7X_SUBLANES = 8
V7X_LANES = 128
V7X_MXU_DIM = 256
V7X_VMEM_LIMIT_BYTES = 56 * 1024 * 1024

NEG_INF = float("-inf")


def _params(*semantics, flags=None):
    return pltpu.CompilerParams(dimension_semantics=semantics,
                                vmem_limit_bytes=V7X_VMEM_LIMIT_BYTES,
                                flags=flags)


def _block(n, target):
    b = min(n, target)
    while n % b:
        b //= 2
    return b


def _gelu(x):
    return 0.5 * x * (1.0 + lax.erf(x * (1.0 / math.sqrt(2.0))))


def _sigmoid(x):
    return 0.5 + 0.5 * jnp.tanh(0.5 * x)


def _rms(x, g):
    return x * lax.rsqrt(jnp.mean(x * x, axis=-1, keepdims=True) + EPS) * g


def _qkv_kernel(x_ref, gmix_ref, w_ref, gq_ref, gk_ref, cos_ref, sin_ref,
                q_ref, k_ref, v_ref):
    tm = x_ref.shape[0]
    n_sub = 2 if tm % (4 * V7X_SUBLANES) == 0 else 1
    halves = [slice(c * (tm // n_sub), (c + 1) * (tm // n_sub)) for c in range(n_sub)]
    zs = [jnp.dot(_rms(x_ref[rows, :], gmix_ref[...]).astype(BF16), w_ref[...], preferred_element_type=F32)
          for rows in halves]
    lane = lax.broadcasted_iota(jnp.int32, (tm // n_sub, HEAD_DIM), 1)
    first_half = (lane % (HEAD_DIM // 2)) < (HEAD_DIM // 4)
    scale = HEAD_DIM ** -0.5 * math.log2(math.e)
    gq = gq_ref[...]
    gk = gk_ref[...]
    for rows, z in zip(halves, zs):
        cos = cos_ref[rows, :]
        sin = sin_ref[rows, :]

        def norm_rope(zh, g):
            y = _rms(zh, g)
            partner = jnp.where(first_half,
                                pltpu.roll(y, HEAD_DIM - HEAD_DIM // 4, 1),
                                pltpu.roll(y, HEAD_DIM // 4, 1))
            return y * cos + partner * sin

        for hd in range(N_HEADS):
            sl = slice(hd * HEAD_DIM, (hd + 1) * HEAD_DIM)
            q_ref[rows, sl] = (norm_rope(z[:, sl], gq) * scale).astype(BF16)
        for hd in range(N_KV_HEADS):
            sl = slice(hd * HEAD_DIM, (hd + 1) * HEAD_DIM)
            k_ref[rows, sl] = norm_rope(z[:, ATTN_WIDTH + hd * HEAD_DIM:ATTN_WIDTH + (hd + 1) * HEAD_DIM], gk).astype(BF16)
        ones = jnp.ones((z.shape[0], HEAD_DIM), BF16)
        for hd in range(N_KV_HEADS):
            v0 = ATTN_WIDTH + KV_WIDTH + hd * HEAD_DIM
            v_ref[rows, 2 * hd * HEAD_DIM:(2 * hd + 1) * HEAD_DIM] = z[:, v0:v0 + HEAD_DIM].astype(BF16)
            v_ref[rows, (2 * hd + 1) * HEAD_DIM:(2 * hd + 2) * HEAD_DIM] = ones


def _qkv_proj(x2, g_mix, w_qkv, g_q, g_k, cos_t, sin_t, seq):
    n = x2.shape[0]
    tm = _block(seq, 512)
    nseq_blocks = seq // tm
    row = lambda i: (i, 0)
    fixed = lambda i: (0, 0)
    pos = lambda i: (i % nseq_blocks, 0)
    return pl.pallas_call(
        _qkv_kernel,
        grid=(n // tm,),
        in_specs=[
            pl.BlockSpec((tm, D_MODEL), row),
            pl.BlockSpec((1, D_MODEL), fixed),
            pl.BlockSpec((D_MODEL, QKV_WIDTH), fixed),
            pl.BlockSpec((1, HEAD_DIM), fixed),
            pl.BlockSpec((1, HEAD_DIM), fixed),
            pl.BlockSpec((tm, HEAD_DIM), pos),
            pl.BlockSpec((tm, HEAD_DIM), pos),
        ],
        out_specs=[
            pl.BlockSpec((tm, ATTN_WIDTH), row),
            pl.BlockSpec((tm, KV_WIDTH), row),
            pl.BlockSpec((tm, 2 * KV_WIDTH), row),
        ],
        out_shape=[
            jax.ShapeDtypeStruct((n, ATTN_WIDTH), BF16),
            jax.ShapeDtypeStruct((n, KV_WIDTH), BF16),
            jax.ShapeDtypeStruct((n, 2 * KV_WIDTH), BF16),
        ],
        compiler_params=_params("parallel"),
        name="qkv_proj",
    )(x2, g_mix, w_qkv, g_q, g_k, cos_t, sin_t)


def _xy_kernel(x_ref, gmix_ref, w_ref, xr_ref, yr_ref):
    h = _rms(x_ref[...], gmix_ref[...]).astype(BF16)
    z = jnp.dot(h, w_ref[...], preferred_element_type=F32)
    xr_ref[...] = z[:, :LRU_WIDTH]
    yr_ref[...] = z[:, LRU_WIDTH:]


def _xy_proj(x2, g_mix, w_xy):
    n = x2.shape[0]
    tm = _block(n, 512)
    row = lambda i: (i, 0)
    fixed = lambda i: (0, 0)
    return pl.pallas_call(
        _xy_kernel,
        grid=(n // tm,),
        in_specs=[
            pl.BlockSpec((tm, D_MODEL), row),
            pl.BlockSpec((1, D_MODEL), fixed),
            pl.BlockSpec((D_MODEL, 2 * LRU_WIDTH), fixed),
        ],
        out_specs=[pl.BlockSpec((tm, LRU_WIDTH), row), pl.BlockSpec((tm, LRU_WIDTH), row)],
        out_shape=[jax.ShapeDtypeStruct((n, LRU_WIDTH), F32)] * 2,
        compiler_params=_params("parallel"),
        name="xy_proj",
    )(x2, g_mix, w_xy)


def _attn_kernel(q_ref, k_ref, v_ref, o_ref, m_scr, acc_scr):
    ki = pl.program_id(3)

    @pl.when(ki == 0)
    def _():
        m_scr[...] = jnp.full(m_scr.shape, NEG_INF, F32)
        acc_scr[...] = jnp.zeros(acc_scr.shape, F32)

    k = k_ref[...]
    v = v_ref[...]
    tk = k.shape[0]
    tq = q_ref.shape[0]
    lane_tiles = tk // HEAD_DIM
    nt = (((1,), (1,)), ((), ()))
    for hd in range(HEADS_PER_KV):
        q = q_ref[:, hd * HEAD_DIM:(hd + 1) * HEAD_DIM]
        if hd == 0 and tk % (2 * V7X_MXU_DIM) == 0:
            s = jnp.concatenate([lax.dot_general(q, k[:tk // 2], nt, preferred_element_type=F32),
                                 lax.dot_general(q, k[tk // 2:], nt, preferred_element_type=F32)], axis=1)
        else:
            s = lax.dot_general(q, k, nt, preferred_element_type=F32)
        m_prev = m_scr[hd]
        m_next = jnp.maximum(m_prev, jnp.max(s, axis=1, keepdims=True))
        p = jnp.exp2(s - jnp.tile(m_next, (1, lane_tiles))).astype(BF16)
        alpha = jnp.exp2(m_prev - m_next)
        if hd == HEADS_PER_KV - 1 and tq % 32 == 0:
            pv = jnp.concatenate([jnp.dot(p[:tq // 2], v, preferred_element_type=F32),
                                  jnp.dot(p[tq // 2:], v, preferred_element_type=F32)], axis=0)
        else:
            pv = jnp.dot(p, v, preferred_element_type=F32)
        acc_scr[hd] = jnp.tile(alpha, (1, 2)) * acc_scr[hd] + pv
        m_scr[hd] = m_next

    @pl.when(ki == pl.num_programs(3) - 1)
    def _():
        for hd in range(HEADS_PER_KV):
            acc = acc_scr[hd]
            o_ref[:, hd * HEAD_DIM:(hd + 1) * HEAD_DIM] = acc[:, :HEAD_DIM] / acc[:, HEAD_DIM:]


def _attention(q, k, v, batch, seq):
    n = q.shape[0]
    tq = _block(seq, 1024)
    tk = _block(seq, 2048)
    nq, nk = seq // tq, seq // tk
    group_w = HEADS_PER_KV * HEAD_DIM
    return pl.pallas_call(
        _attn_kernel,
        grid=(batch, N_KV_HEADS, nq, nk),
        in_specs=[
            pl.BlockSpec((tq, group_w), lambda b, g, qi, ki: (b * nq + qi, g)),
            pl.BlockSpec((tk, HEAD_DIM), lambda b, g, qi, ki: (b * nk + ki, g)),
            pl.BlockSpec((tk, 2 * HEAD_DIM), lambda b, g, qi, ki: (b * nk + ki, g)),
        ],
        out_specs=pl.BlockSpec((tq, group_w), lambda b, g, qi, ki: (b * nq + qi, g)),
        out_shape=jax.ShapeDtypeStruct((n, ATTN_WIDTH), F32),
        scratch_shapes=[
            pltpu.VMEM((HEADS_PER_KV, tq, HEAD_DIM), F32),
            pltpu.VMEM((HEADS_PER_KV, tq, 2 * HEAD_DIM), F32),
        ],
        compiler_params=_params("parallel", "parallel", "parallel", "arbitrary"),
        name="flash_attention",
    )(q, k, v)


def _lru_kernel(reverse, tc, n_chunks, *refs):
    if reverse:
        (xm_ref, xp_ref, xn_ref, cw_ref, cb_ref, wg_ref, bg_ref, lam_ref, hf_ref, yr_ref,
         out_ref, ext_scr, a_scr, u_scr, carry_scr) = refs
    else:
        (xm_ref, xp_ref, xn_ref, cw_ref, cb_ref, wg_ref, bg_ref, lam_ref,
         out_ref, ext_scr, a_scr, u_scr, carry_scr) = refs
    c = pl.program_id(1)
    chunk = (n_chunks - 1 - c) if reverse else c
    halo = V7X_SUBLANES

    @pl.when(c == 0)
    def _():
        carry_scr[...] = jnp.zeros(carry_scr.shape, F32)

    ext_scr[0:halo, :] = jnp.where(chunk == 0, 0.0, xp_ref[...])
    ext_scr[halo:halo + tc, :] = xm_ref[...]
    ext_scr[halo + tc:2 * halo + tc, :] = jnp.where(chunk == n_chunks - 1, 0.0, xn_ref[...])
    xc = cb_ref[...]
    for tap in range(CONV_W):
        start = halo + tap - CONV_PAD_LEFT
        xc = xc + ext_scr[start:start + tc, :] * cw_ref[tap:tap + 1, :]

    xcb = xc.astype(BF16)
    za, zx = [], []
    for g in range(LRU_WIDTH // V7X_MXU_DIM):
        zg = jnp.dot(xcb[:, g * V7X_MXU_DIM:(g + 1) * V7X_MXU_DIM], wg_ref[g],
                     preferred_element_type=F32)
        za.append(zg[:, :V7X_MXU_DIM])
        zx.append(zg[:, V7X_MXU_DIM:])
    r = _sigmoid(jnp.concatenate(za, axis=1) + bg_ref[0:1, :])
    i = _sigmoid(jnp.concatenate(zx, axis=1) + bg_ref[1:2, :])
    lam = lam_ref[...]
    softplus_neg = jnp.maximum(-lam, 0.0) + jnp.log(1.0 + jnp.exp(-jnp.abs(lam)))
    a = jnp.exp((-LRU_C * softplus_neg) * r)
    u = jnp.sqrt(1.0 - a * a) * (i * xc)

    nt = tc // V7X_SUBLANES
    a3 = a.reshape(nt, V7X_SUBLANES, LRU_WIDTH)
    u3 = u.reshape(nt, V7X_SUBLANES, LRU_WIDTH)
    row = lax.broadcasted_iota(jnp.int32, a3.shape, 1)
    for s in (1, 2, 4):
        if reverse:
            shift, valid = V7X_SUBLANES - s, row < V7X_SUBLANES - s
        else:
            shift, valid = s, row >= s
        a_sh = pltpu.roll(a3, shift, 1)
        u_sh = pltpu.roll(u3, shift, 1)
        u3 = u3 + jnp.where(valid, a3 * u_sh, 0.0)
        a3 = jnp.where(valid, a3 * a_sh, a3)
    a_scr[...] = a3
    u_scr[...] = u3

    def tile_step(t, carry):
        tt = (nt - 1 - t) if reverse else t
        h = u_scr[tt] + a_scr[tt] * carry
        u_scr[tt] = h
        edge = h[0:1, :] if reverse else h[V7X_SUBLANES - 1:V7X_SUBLANES, :]
        return jnp.broadcast_to(edge, h.shape)

    carry_scr[...] = lax.fori_loop(0, nt, tile_step, carry_scr[...])
    h_all = u_scr[...].reshape(tc, LRU_WIDTH)
    if reverse:
        out_ref[...] = (h_all + hf_ref[...]) * _gelu(yr_ref[...])
    else:
        out_ref[...] = h_all


def _lru_direction(reverse, xr, conv_w, conv_b, w_gates, b_gates, lam, batch, seq, h_fwd=None, yr=None):
    n = xr.shape[0]
    tc = _block(seq, 256)
    n_chunks = seq // tc
    halo = V7X_SUBLANES
    per_tile = tc // halo
    n_tiles = n // halo

    def chunk_of(c):
        return (n_chunks - 1 - c) if reverse else c

    main = lambda b, c: (b * n_chunks + chunk_of(c), 0)
    prev = lambda b, c: (jnp.maximum((b * n_chunks + chunk_of(c)) * per_tile - 1, 0), 0)
    nxt = lambda b, c: (jnp.minimum((b * n_chunks + chunk_of(c) + 1) * per_tile, n_tiles - 1), 0)
    fixed2 = lambda b, c: (0, 0)
    fixed3 = lambda b, c: (0, 0, 0)
    in_specs = [
        pl.BlockSpec((tc, LRU_WIDTH), main),
        pl.BlockSpec((halo, LRU_WIDTH), prev),
        pl.BlockSpec((halo, LRU_WIDTH), nxt),
        pl.BlockSpec((CONV_W, LRU_WIDTH), fixed2),
        pl.BlockSpec((1, LRU_WIDTH), fixed2),
        pl.BlockSpec((LRU_WIDTH // V7X_MXU_DIM, V7X_MXU_DIM, 2 * V7X_MXU_DIM), fixed3),
        pl.BlockSpec((2, LRU_WIDTH), fixed2),
        pl.BlockSpec((1, LRU_WIDTH), fixed2),
    ]
    args = [xr, xr, xr, conv_w, conv_b, w_gates, b_gates, lam]
    if reverse:
        in_specs += [pl.BlockSpec((tc, LRU_WIDTH), main), pl.BlockSpec((tc, LRU_WIDTH), main)]
        args += [h_fwd, yr]
    return pl.pallas_call(
        functools.partial(_lru_kernel, reverse, tc, n_chunks),
        grid=(batch, n_chunks),
        in_specs=in_specs,
        out_specs=pl.BlockSpec((tc, LRU_WIDTH), main),
        out_shape=jax.ShapeDtypeStruct((n, LRU_WIDTH), F32),
        scratch_shapes=[
            pltpu.VMEM((tc + 2 * halo, LRU_WIDTH), F32),
            pltpu.VMEM((per_tile, halo, LRU_WIDTH), F32),
            pltpu.VMEM((per_tile, halo, LRU_WIDTH), F32),
            pltpu.VMEM((halo, LRU_WIDTH), F32),
        ],
        compiler_params=_params("parallel", "arbitrary"),
        name="lru_bwd" if reverse else "lru_fwd",
    )(*args)


def _out_kernel(x_ref, attn_ref, lru_ref, ga_ref, gl_ref, wa_ref, wl_ref, gf_ref, wq_ref,
                x1_ref, xnt_ref, qp_ref):
    an = _rms(attn_ref[...], ga_ref[...]).astype(BF16)
    ln = _rms(lru_ref[...], gl_ref[...]).astype(BF16)
    x1 = (x_ref[...] + jnp.dot(an, wa_ref[...], preferred_element_type=F32)
          + jnp.dot(ln, wl_ref[...], preferred_element_type=F32))
    x1_ref[...] = x1
    xn = _rms(x1, gf_ref[...])
    xnt_ref[...] = xn.T.astype(BF16)
    qp_ref[...] = jnp.dot(xn.astype(BF16), wq_ref[...], preferred_element_type=F32)


def _out_proj(x2, attn, lru, g_attn, g_lru, w_out_a, w_out_l, g_ffn, w_query):
    n = x2.shape[0]
    tm = _block(n, 512)
    qw = PEER_HEADS * PEER_KEY_DIM
    row = lambda i: (i, 0)
    fixed = lambda i: (0, 0)
    return pl.pallas_call(
        _out_kernel,
        grid=(n // tm,),
        in_specs=[
            pl.BlockSpec((tm, D_MODEL), row),
            pl.BlockSpec((tm, ATTN_WIDTH), row),
            pl.BlockSpec((tm, LRU_WIDTH), row),
            pl.BlockSpec((1, ATTN_WIDTH), fixed),
            pl.BlockSpec((1, LRU_WIDTH), fixed),
            pl.BlockSpec((ATTN_WIDTH, D_MODEL), fixed),
            pl.BlockSpec((LRU_WIDTH, D_MODEL), fixed),
            pl.BlockSpec((1, D_MODEL), fixed),
            pl.BlockSpec((D_MODEL, qw), fixed),
        ],
        out_specs=[
            pl.BlockSpec((tm, D_MODEL), row),
            pl.BlockSpec((D_MODEL, tm), lambda i: (0, i)),
            pl.BlockSpec((tm, qw), row),
        ],
        out_shape=[
            jax.ShapeDtypeStruct((n, D_MODEL), F32),
            jax.ShapeDtypeStruct((D_MODEL, n), BF16),
            jax.ShapeDtypeStruct((n, qw), F32),
        ],
        compiler_params=_params("parallel"),
        name="out_proj",
    )(x2, attn, lru, g_attn, g_lru, w_out_a, w_out_l, g_ffn, w_query)


def _top_values(s, count, with_rank=False):
    vals = []
    rank = jnp.full(s.shape, float(count), F32) if with_rank else None
    for r in range(count):
        m = jnp.max(s, axis=0, keepdims=True)
        vals.append(m)
        hit = s == m
        if with_rank:
            rank = jnp.minimum(rank, jnp.where(hit, float(r), float(count)))
        s = jnp.where(hit, NEG_INF, s)
    return (vals, rank) if with_rank else vals


def _peer_score_kernel(qp_ref, keys_ref, n1_ref, e1_ref, c2_ref, e2_ref):
    half_tile = V7X_SUBLANES
    sub = lax.broadcasted_iota(jnp.int32, (half_tile, qp_ref.shape[0]), 0)
    for hd in range(PEER_HEADS):
        qh = qp_ref[:, hd * PEER_KEY_DIM:(hd + 1) * PEER_KEY_DIM].astype(BF16)
        nt = (((1,), (1,)), ((), ()))
        s1 = lax.dot_general(keys_ref[hd, 0], qh, nt, preferred_element_type=F32)
        s2 = lax.dot_general(keys_ref[hd, 1], qh, nt, preferred_element_type=F32)
        top1 = _top_values(s1, PEER_TOPK)
        top2_rows, rank2 = _top_values(s2, PEER_TOPK, with_rank=True)
        top2 = jnp.concatenate(top2_rows, axis=0)
        cands = [top1[0] + top2]
        for r in range(1, PEER_TOPK):
            ok = sub < PEER_TOPK // (r + 1)
            cands.append(jnp.where(ok, top1[r] + top2[:half_tile, :], NEG_INF))
        tau = _top_values(jnp.concatenate(cands, axis=0), PEER_TOPK)[-1]
        m12 = top1[0] + top2[0:1, :]
        z = jnp.zeros_like(tau)
        n1 = jnp.zeros_like(s1)
        for r in range(PEER_TOPK):
            picked = cands[r] >= tau
            z = z + jnp.sum(jnp.where(picked, jnp.exp(cands[r] - m12), 0.0), axis=0, keepdims=True)
            n_r = jnp.sum(jnp.where(picked, 1.0, 0.0), axis=0, keepdims=True)
            n1 = jnp.where(s1 == top1[r], n_r, n1)
        n1_ref[hd] = n1
        e1_ref[hd] = jnp.exp(s1 - top1[0]) / z
        c2_ref[hd] = rank2.astype(BF16)
        e2_ref[hd] = jnp.exp(s2 - top2[0:1, :]).astype(BF16)


def _peer_scores(qp, keys_pad):
    n = qp.shape[0]
    tb = _block(n, 256)
    qw = PEER_HEADS * PEER_KEY_DIM
    big = pl.BlockSpec((PEER_HEADS, N_KEYS, tb), lambda i: (0, 0, i))
    shape_f32 = jax.ShapeDtypeStruct((PEER_HEADS, N_KEYS, n), F32)
    shape_bf16 = jax.ShapeDtypeStruct((PEER_HEADS, N_KEYS, n), BF16)
    return pl.pallas_call(
        _peer_score_kernel,
        grid=(n // tb,),
        in_specs=[
            pl.BlockSpec((tb, qw), lambda i: (i, 0)),
            pl.BlockSpec((PEER_HEADS, 2, N_KEYS, PEER_KEY_DIM), lambda i: (0, 0, 0, 0)),
        ],
        out_specs=[big, big, big, big],
        out_shape=[shape_f32, shape_f32, shape_bf16, shape_bf16],
        compiler_params=_params("parallel"),
        name="peer_scores",
    )(qp, keys_pad)


PEER_KEYS_PER_STEP = V7X_SUBLANES
PEER_EXPERTS_PER_STEP = PEER_KEYS_PER_STEP * N_KEYS


def _peer_gate(n1_ref, e1_ref, c2_ref, e2_ref, g_scr):
    tb = g_scr.shape[1]
    bf16_rows = 2 * V7X_SUBLANES

    def key_row(ref, hd, il):
        row = jnp.broadcast_to(ref[hd, il:il + 1, :], (bf16_rows, tb)).astype(BF16)
        return jnp.tile(row, (N_KEYS // bf16_rows, 1))

    for il in range(PEER_KEYS_PER_STEP):
        gate = jnp.zeros((N_KEYS, tb), BF16)
        for hd in range(PEER_HEADS):
            n_row = key_row(n1_ref, hd, il)
            gate = gate + jnp.where(c2_ref[hd] < n_row, e2_ref[hd], 0.0) * key_row(e1_ref, hd, il)
        g_scr[il * N_KEYS:(il + 1) * N_KEYS, :] = gate


PEER_FIRST_MATMULS = 4


def _peer_mix_kernel(xnt_ref, ed_ref, eu_ref, n1_ref, e1_ref, c2_ref, e2_ref, x1_ref, y_ref, g_scr, w_scr):
    ec = pl.program_id(1)

    @pl.when(ec == 0)
    def _():
        y_ref[...] = x1_ref[...]

    _peer_gate(n1_ref, e1_ref, c2_ref, e2_ref, g_scr)
    tn = (((0,), (0,)), ((), ()))
    n_h = PEER_FIRST_MATMULS
    piece = ed_ref.shape[0] // n_h
    pieces = [slice(c * piece, (c + 1) * piece) for c in range(n_h)]
    hids = [jnp.dot(ed_ref[rows, :], xnt_ref[...], preferred_element_type=F32) for rows in pieces]
    parts = []
    for c in range(2):
        for q in range(c * n_h // 2, (c + 1) * n_h // 2):
            w_scr[pieces[q], :] = _gelu(hids[q].astype(BF16)) * g_scr[pieces[q], :]
        rows = slice(c * (n_h // 2) * piece, (c + 1) * (n_h // 2) * piece)
        parts.append(lax.dot_general(w_scr[rows, :], eu_ref[rows, :], tn, preferred_element_type=F32))
    y_ref[...] += parts[0] + parts[1]


def _peer_mix(xnt, e_down, e_up, n1, e1, c2, e2, x1):
    n = xnt.shape[1]
    tb = _block(n, 512)
    ec = PEER_EXPERTS_PER_STEP
    tok = lambda t, e: (t, 0)
    exp_ = lambda t, e: (e, 0)
    key1 = pl.BlockSpec((PEER_HEADS, PEER_KEYS_PER_STEP, tb), lambda t, e: (0, e, t))
    key2 = pl.BlockSpec((PEER_HEADS, N_KEYS, tb), lambda t, e: (0, 0, t))
    return pl.pallas_call(
        _peer_mix_kernel,
        grid=(n // tb, N_EXPERTS // ec),
        in_specs=[
            pl.BlockSpec((D_MODEL, tb), lambda t, e: (0, t)),
            pl.BlockSpec((ec, D_MODEL), exp_),
            pl.BlockSpec((ec, D_MODEL), exp_),
            key1, key1, key2, key2,
            pl.BlockSpec((tb, D_MODEL), tok),
        ],
        out_specs=pl.BlockSpec((tb, D_MODEL), tok),
        out_shape=jax.ShapeDtypeStruct((n, D_MODEL), F32),
        scratch_shapes=[pltpu.VMEM((ec, tb), BF16), pltpu.VMEM((ec, tb), BF16)],
        compiler_params=_params("parallel", "arbitrary"),
        name="peer_mix",
    )(xnt, e_down, e_up, n1, e1, c2, e2, x1)


def _rope_tables(seq):
    t = jnp.arange(seq, dtype=jnp.int32)
    row = (t // GRID_W).astype(F32)
    col = (t % GRID_W).astype(F32)
    half = HEAD_DIM // 2
    inv = ROPE_THETA ** (-jnp.arange(0, half, 2, dtype=F32) / half)
    ar = row[:, None] * inv
    ac = col[:, None] * inv
    cos_t = jnp.concatenate([jnp.cos(ar), jnp.cos(ar), jnp.cos(ac), jnp.cos(ac)], axis=-1)
    sin_t = jnp.concatenate([-jnp.sin(ar), jnp.sin(ar), -jnp.sin(ac), jnp.sin(ac)], axis=-1)
    return cos_t, sin_t


def _gate_weights(w_gate_a, w_gate_x):
    per_tile = V7X_MXU_DIM // LRU_BLOCK_W
    n_tiles = LRU_WIDTH // V7X_MXU_DIM

    def tiles(w):
        w = w.reshape(n_tiles, per_tile, LRU_BLOCK_W, LRU_BLOCK_W)
        eye = jnp.eye(per_tile, dtype=w.dtype)
        return jnp.einsum("tpkj,pq->tpkqj", w, eye).reshape(n_tiles, V7X_MXU_DIM, V7X_MXU_DIM)

    return jnp.concatenate([tiles(w_gate_a), tiles(w_gate_x)], axis=-1).astype(BF16)


def _padded_keys(sub_keys):
    z = jnp.zeros_like(sub_keys[:, 0])
    k0 = jnp.concatenate([sub_keys[:, 0], z], axis=-1)
    k1 = jnp.concatenate([z, sub_keys[:, 1]], axis=-1)
    return jnp.stack([k0, k1], axis=1).astype(BF16)


def _layer(x2, p, batch, seq, cos_t, sin_t):
    q, k, v = _qkv_proj(x2, p["g_mix"], p["w_qkv"], p["g_q"], p["g_k"], cos_t, sin_t, seq)
    attn = _attention(q, k, v, batch, seq)
    xr, yr = _xy_proj(x2, p["g_mix"], p["w_xy"])
    h_fwd = _lru_direction(False, xr, p["conv_w"], p["conv_b"], p["w_gates"][0], p["b_gates"][0],
                           p["lam"][0], batch, seq)
    lru = _lru_direction(True, xr, p["conv_w"], p["conv_b"], p["w_gates"][1], p["b_gates"][1],
                         p["lam"][1], batch, seq, h_fwd=h_fwd, yr=yr)
    x1, xnt, qp = _out_proj(x2, attn, lru, p["g_attn_out"], p["g_lru_out"], p["w_out_a"], p["w_out_l"],
                           p["g_ffn"], p["w_query"])
    n1, e1, c2, e2 = _peer_scores(qp, p["keys_pad"])
    return _peer_mix(xnt, p["e_down"], p["e_up"], n1, e1, c2, e2, x1)


def _trunk(x, layers):
    batch, seq, _ = x.shape
    x2 = x.reshape(batch * seq, D_MODEL)
    cos_t, sin_t = _rope_tables(seq)
    for p in layers:
        x2 = _layer(x2, p, batch, seq, cos_t, sin_t)
    return x2.reshape(batch, seq, D_MODEL)


def _layer_params(l, g_mix, w_in, g_q, g_k, conv_w, conv_b, w_gate_a, b_gate_a, w_gate_x, b_gate_x, lru_lambda,
                  g_attn_out, g_lru_out, w_out, g_ffn, w_query, sub_keys, expert_down, expert_up):
    w_in_b = w_in[l].astype(BF16)
    w_out_b = w_out[l].astype(BF16)
    return {
        "g_mix": g_mix[l][None, :],
        "w_qkv": w_in_b[:, :QKV_WIDTH],
        "w_xy": w_in_b[:, QKV_WIDTH:],
        "g_q": g_q[l][None, :],
        "g_k": g_k[l][None, :],
        "conv_w": conv_w[l],
        "conv_b": conv_b[l][None, :],
        "w_gates": [_gate_weights(w_gate_a[l, d], w_gate_x[l, d]) for d in range(2)],
        "b_gates": [jnp.stack([b_gate_a[l, d], b_gate_x[l, d]], axis=0) for d in range(2)],
        "lam": [lru_lambda[l, d][None, :] for d in range(2)],
        "g_attn_out": g_attn_out[l][None, :],
        "g_lru_out": g_lru_out[l][None, :],
        "w_out_a": w_out_b[:ATTN_WIDTH],
        "w_out_l": w_out_b[ATTN_WIDTH:],
        "g_ffn": g_ffn[l][None, :],
        "w_query": w_query[l].astype(BF16),
        "keys_pad": _padded_keys(sub_keys[l]),
        "e_down": expert_down[l].astype(BF16),
        "e_up": expert_up[l].astype(BF16),
    }


def kernel(x_prompt, x_sample, g_mix, w_in, g_q, g_k, conv_w, conv_b, w_gate_a, b_gate_a, w_gate_x, b_gate_x, lru_lambda, g_attn_out, g_lru_out, w_out, g_ffn, w_query, sub_keys, expert_down, expert_up):
    weights = (g_mix, w_in, g_q, g_k, conv_w, conv_b, w_gate_a, b_gate_a, w_gate_x, b_gate_x, lru_lambda,
               g_attn_out, g_lru_out, w_out, g_ffn, w_query, sub_keys, expert_down, expert_up)
    layers = [_layer_params(l, *weights) for l in range(w_in.shape[0])]
    return (_trunk(x_prompt, layers), _trunk(x_sample, layers))
```

```python
import functools
import math

import jax
import jax.numpy as jnp
from jax import lax
from jax.experimental import pallas as pl
from jax.experimental.pallas import tpu as pltpu

F32 = jnp.float32
BF16 = jnp.bfloat16

D_MODEL = 2048
GRID_W = 64
N_HEADS = 8
N_KV_HEADS = 2
HEAD_DIM = 128
HEADS_PER_KV = N_HEADS // N_KV_HEADS
ATTN_WIDTH = N_HEADS * HEAD_DIM
KV_WIDTH = N_KV_HEADS * HEAD_DIM
QKV_WIDTH = ATTN_WIDTH + 2 * KV_WIDTH
ROPE_THETA = 10000.0
LRU_WIDTH = D_MODEL - ATTN_WIDTH
LRU_BLOCK_W = 64
CONV_W = 4
CONV_PAD_LEFT = 2
LRU_C = 8.0
N_KEYS = 128
N_EXPERTS = N_KEYS * N_KEYS
PEER_HEADS = 8
PEER_KEY_DIM = 128
PEER_HALF = PEER_KEY_DIM // 2
PEER_TOPK = 16
EPS = 1e-6

V7X_SUBLANES = 8
V7X_LANES = 128
V7X_MXU_DIM = 256
V7X_VMEM_LIMIT_BYTES = 56 * 1024 * 1024

NEG_INF = float("-inf")


def _params(*semantics, flags=None):
    return pltpu.CompilerParams(dimension_semantics=semantics,
                                vmem_limit_bytes=V7X_VMEM_LIMIT_BYTES,
                                flags=flags)


def _block(n, target):
    b = min(n, target)
    while n % b:
        b //= 2
    return b


def _gelu(x):
    return 0.5 * x * (1.0 + lax.erf(x * (1.0 / math.sqrt(2.0))))


def _sigmoid(x):
    return 0.5 + 0.5 * jnp.tanh(0.5 * x)


def _rms(x, g):
    return x * lax.rsqrt(jnp.mean(x * x, axis=-1, keepdims=True) + EPS) * g


def _qkv_kernel(x_ref, gmix_ref, w_ref, gq_ref, gk_ref, cos_ref, sin_ref,
                q_ref, k_ref, v_ref):
    tm = x_ref.shape[0]
    n_sub = 2 if tm % (4 * V7X_SUBLANES) == 0 else 1
    halves = [slice(c * (tm // n_sub), (c + 1) * (tm // n_sub)) for c in range(n_sub)]
    zs = [jnp.dot(_rms(x_ref[rows, :], gmix_ref[...]).astype(BF16), w_ref[...], preferred_element_type=F32)
          for rows in halves]
    lane = lax.broadcasted_iota(jnp.int32, (tm // n_sub, HEAD_DIM), 1)
    first_half = (lane % (HEAD_DIM // 2)) < (HEAD_DIM // 4)
    scale = HEAD_DIM ** -0.5 * math.log2(math.e)
    gq = gq_ref[...]
    gk = gk_ref[...]
    for rows, z in zip(halves, zs):
        cos = cos_ref[rows, :]
        sin = sin_ref[rows, :]

        def norm_rope(zh, g):
            y = _rms(zh, g)
            partner = jnp.where(first_half,
                                pltpu.roll(y, HEAD_DIM - HEAD_DIM // 4, 1),
                                pltpu.roll(y, HEAD_DIM // 4, 1))
            return y * cos + partner * sin

        for hd in range(N_HEADS):
            sl = slice(hd * HEAD_DIM, (hd + 1) * HEAD_DIM)
            q_ref[rows, sl] = (norm_rope(z[:, sl], gq) * scale).astype(BF16)
        for hd in range(N_KV_HEADS):
            sl = slice(hd * HEAD_DIM, (hd + 1) * HEAD_DIM)
            k_ref[rows, sl] = norm_rope(z[:, ATTN_WIDTH + hd * HEAD_DIM:ATTN_WIDTH + (hd + 1) * HEAD_DIM], gk).astype(BF16)
        ones = jnp.ones((z.shape[0], HEAD_DIM), BF16)
        for hd in range(N_KV_HEADS):
            v0 = ATTN_WIDTH + KV_WIDTH + hd * HEAD_DIM
            v_ref[rows, 2 * hd * HEAD_DIM:(2 * hd + 1) * HEAD_DIM] = z[:, v0:v0 + HEAD_DIM].astype(BF16)
            v_ref[rows, (2 * hd + 1) * HEAD_DIM:(2 * hd + 2) * HEAD_DIM] = ones


def _qkv_proj(x2, g_mix, w_qkv, g_q, g_k, cos_t, sin_t, seq):
    n = x2.shape[0]
    tm = _block(seq, 512)
    nseq_blocks = seq // tm
    row = lambda i: (i, 0)
    fixed = lambda i: (0, 0)
    pos = lambda i: (i % nseq_blocks, 0)
    return pl.pallas_call(
        _qkv_kernel,
        grid=(n // tm,),
        in_specs=[
            pl.BlockSpec((tm, D_MODEL), row),
            pl.BlockSpec((1, D_MODEL), fixed),
            pl.BlockSpec((D_MODEL, QKV_WIDTH), fixed),
            pl.BlockSpec((1, HEAD_DIM), fixed),
            pl.BlockSpec((1, HEAD_DIM), fixed),
            pl.BlockSpec((tm, HEAD_DIM), pos),
            pl.BlockSpec((tm, HEAD_DIM), pos),
        ],
        out_specs=[
            pl.BlockSpec((tm, ATTN_WIDTH), row),
            pl.BlockSpec((tm, KV_WIDTH), row),
            pl.BlockSpec((tm, 2 * KV_WIDTH), row),
        ],
        out_shape=[
            jax.ShapeDtypeStruct((n, ATTN_WIDTH), BF16),
            jax.ShapeDtypeStruct((n, KV_WIDTH), BF16),
            jax.ShapeDtypeStruct((n, 2 * KV_WIDTH), BF16),
        ],
        compiler_params=_params("parallel"),
        name="qkv_proj",
    )(x2, g_mix, w_qkv, g_q, g_k, cos_t, sin_t)


def _xy_kernel(x_ref, gmix_ref, w_ref, xr_ref, yr_ref):
    h = _rms(x_ref[...], gmix_ref[...]).astype(BF16)
    z = jnp.dot(h, w_ref[...], preferred_element_type=F32)
    xr_ref[...] = z[:, :LRU_WIDTH]
    yr_ref[...] = z[:, LRU_WIDTH:]


def _xy_proj(x2, g_mix, w_xy):
    n = x2.shape[0]
    tm = _block(n, 512)
    row = lambda i: (i, 0)
    fixed = lambda i: (0, 0)
    return pl.pallas_call(
        _xy_kernel,
        grid=(n // tm,),
        in_specs=[
            pl.BlockSpec((tm, D_MODEL), row),
            pl.BlockSpec((1, D_MODEL), fixed),
            pl.BlockSpec((D_MODEL, 2 * LRU_WIDTH), fixed),
        ],
        out_specs=[pl.BlockSpec((tm, LRU_WIDTH), row), pl.BlockSpec((tm, LRU_WIDTH), row)],
        out_shape=[jax.ShapeDtypeStruct((n, LRU_WIDTH), F32)] * 2,
        compiler_params=_params("parallel"),
        name="xy_proj",
    )(x2, g_mix, w_xy)


def _attn_kernel(q_ref, k_ref, v_ref, o_ref, m_scr, acc_scr):
    ki = pl.program_id(3)

    @pl.when(ki == 0)
    def _():
        m_scr[...] = jnp.full(m_scr.shape, NEG_INF, F32)
        acc_scr[...] = jnp.zeros(acc_scr.shape, F32)

    k = k_ref[...]
    v = v_ref[...]
    tk = k.shape[0]
    tq = q_ref.shape[0]
    lane_tiles = tk // HEAD_DIM
    nt = (((1,), (1,)), ((), ()))
    for hd in range(HEADS_PER_KV):
        q = q_ref[:, hd * HEAD_DIM:(hd + 1) * HEAD_DIM]
        if hd == 0 and tk % (2 * V7X_MXU_DIM) == 0:
            s = jnp.concatenate([lax.dot_general(q, k[:tk // 2], nt, preferred_element_type=F32),
                                 lax.dot_general(q, k[tk // 2:], nt, preferred_element_type=F32)], axis=1)
        else:
            s = lax.dot_general(q, k, nt, preferred_element_type=F32)
        m_prev = m_scr[hd]
        m_next = jnp.maximum(m_prev, jnp.max(s, axis=1, keepdims=True))
        p = jnp.exp2(s - jnp.tile(m_next, (1, lane_tiles))).astype(BF16)
        alpha = jnp.exp2(m_prev - m_next)
        if hd == HEADS_PER_KV - 1 and tq % 32 == 0:
            pv = jnp.concatenate([jnp.dot(p[:tq // 2], v, preferred_element_type=F32),
                                  jnp.dot(p[tq // 2:], v, preferred_element_type=F32)], axis=0)
        else:
            pv = jnp.dot(p, v, preferred_element_type=F32)
        acc_scr[hd] = jnp.tile(alpha, (1, 2)) * acc_scr[hd] + pv
        m_scr[hd] = m_next

    @pl.when(ki == pl.num_programs(3) - 1)
    def _():
        for hd in range(HEADS_PER_KV):
            acc = acc_scr[hd]
            o_ref[:, hd * HEAD_DIM:(hd + 1) * HEAD_DIM] = acc[:, :HEAD_DIM] / acc[:, HEAD_DIM:]


def _attention(q, k, v, batch, seq):
    n = q.shape[0]
    tq = _block(seq, 1024)
    tk = _block(seq, 2048)
    nq, nk = seq // tq, seq // tk
    group_w = HEADS_PER_KV * HEAD_DIM
    return pl.pallas_call(
        _attn_kernel,
        grid=(batch, N_KV_HEADS, nq, nk),
        in_specs=[
            pl.BlockSpec((tq, group_w), lambda b, g, qi, ki: (b * nq + qi, g)),
            pl.BlockSpec((tk, HEAD_DIM), lambda b, g, qi, ki: (b * nk + ki, g)),
            pl.BlockSpec((tk, 2 * HEAD_DIM), lambda b, g, qi, ki: (b * nk + ki, g)),
        ],
        out_specs=pl.BlockSpec((tq, group_w), lambda b, g, qi, ki: (b * nq + qi, g)),
        out_shape=jax.ShapeDtypeStruct((n, ATTN_WIDTH), F32),
        scratch_shapes=[
            pltpu.VMEM((HEADS_PER_KV, tq, HEAD_DIM), F32),
            pltpu.VMEM((HEADS_PER_KV, tq, 2 * HEAD_DIM), F32),
        ],
        compiler_params=_params("parallel", "parallel", "parallel", "arbitrary"),
        name="flash_attention",
    )(q, k, v)


def _lru_kernel(reverse, tc, n_chunks, *refs):
    if reverse:
        (xm_ref, xp_ref, xn_ref, cw_ref, cb_ref, wg_ref, bg_ref, lam_ref, hf_ref, yr_ref,
         out_ref, ext_scr, a_scr, u_scr, carry_scr) = refs
    else:
        (xm_ref, xp_ref, xn_ref, cw_ref, cb_ref, wg_ref, bg_ref, lam_ref,
         out_ref, ext_scr, a_scr, u_scr, carry_scr) = refs
    c = pl.program_id(1)
    chunk = (n_chunks - 1 - c) if reverse else c
    halo = V7X_SUBLANES

    @pl.when(c == 0)
    def _():
        carry_scr[...] = jnp.zeros(carry_scr.shape, F32)

    ext_scr[0:halo, :] = jnp.where(chunk == 0, 0.0, xp_ref[...])
    ext_scr[halo:halo + tc, :] = xm_ref[...]
    ext_scr[halo + tc:2 * halo + tc, :] = jnp.where(chunk == n_chunks - 1, 0.0, xn_ref[...])
    xc = cb_ref[...]
    for tap in range(CONV_W):
        start = halo + tap - CONV_PAD_LEFT
        xc = xc + ext_scr[start:start + tc, :] * cw_ref[tap:tap + 1, :]

    xcb = xc.astype(BF16)
    za, zx = [], []
    for g in range(LRU_WIDTH // V7X_MXU_DIM):
        zg = jnp.dot(xcb[:, g * V7X_MXU_DIM:(g + 1) * V7X_MXU_DIM], wg_ref[g],
                     preferred_element_type=F32)
        za.append(zg[:, :V7X_MXU_DIM])
        zx.append(zg[:, V7X_MXU_DIM:])
    r = _sigmoid(jnp.concatenate(za, axis=1) + bg_ref[0:1, :])
    i = _sigmoid(jnp.concatenate(zx, axis=1) + bg_ref[1:2, :])
    lam = lam_ref[...]
    softplus_neg = jnp.maximum(-lam, 0.0) + jnp.log(1.0 + jnp.exp(-jnp.abs(lam)))
    a = jnp.exp((-LRU_C * softplus_neg) * r)
    u = jnp.sqrt(1.0 - a * a) * (i * xc)

    nt = tc // V7X_SUBLANES
    a3 = a.reshape(nt, V7X_SUBLANES, LRU_WIDTH)
    u3 = u.reshape(nt, V7X_SUBLANES, LRU_WIDTH)
    row = lax.broadcasted_iota(jnp.int32, a3.shape, 1)
    for s in (1, 2, 4):
        if reverse:
            shift, valid = V7X_SUBLANES - s, row < V7X_SUBLANES - s
        else:
            shift, valid = s, row >= s
        a_sh = pltpu.roll(a3, shift, 1)
        u_sh = pltpu.roll(u3, shift, 1)
        u3 = u3 + jnp.where(valid, a3 * u_sh, 0.0)
        a3 = jnp.where(valid, a3 * a_sh, a3)
    a_scr[...] = a3
    u_scr[...] = u3

    def tile_step(t, carry):
        tt = (nt - 1 - t) if reverse else t
        h = u_scr[tt] + a_scr[tt] * carry
        u_scr[tt] = h
        edge = h[0:1, :] if reverse else h[V7X_SUBLANES - 1:V7X_SUBLANES, :]
        return jnp.broadcast_to(edge, h.shape)

    carry_scr[...] = lax.fori_loop(0, nt, tile_step, carry_scr[...])
    h_all = u_scr[...].reshape(tc, LRU_WIDTH)
    if reverse:
        out_ref[...] = (h_all + hf_ref[...]) * _gelu(yr_ref[...])
    else:
        out_ref[...] = h_all


def _lru_direction(reverse, xr, conv_w, conv_b, w_gates, b_gates, lam, batch, seq, h_fwd=None, yr=None):
    n = xr.shape[0]
    tc = _block(seq, 256)
    n_chunks = seq // tc
    halo = V7X_SUBLANES
    per_tile = tc // halo
    n_tiles = n // halo

    def chunk_of(c):
        return (n_chunks - 1 - c) if reverse else c

    main = lambda b, c: (b * n_chunks + chunk_of(c), 0)
    prev = lambda b, c: (jnp.maximum((b * n_chunks + chunk_of(c)) * per_tile - 1, 0), 0)
    nxt = lambda b, c: (jnp.minimum((b * n_chunks + chunk_of(c) + 1) * per_tile, n_tiles - 1), 0)
    fixed2 = lambda b, c: (0, 0)
    fixed3 = lambda b, c: (0, 0, 0)
    in_specs = [
        pl.BlockSpec((tc, LRU_WIDTH), main),
        pl.BlockSpec((halo, LRU_WIDTH), prev),
        pl.BlockSpec((halo, LRU_WIDTH), nxt),
        pl.BlockSpec((CONV_W, LRU_WIDTH), fixed2),
        pl.BlockSpec((1, LRU_WIDTH), fixed2),
        pl.BlockSpec((LRU_WIDTH // V7X_MXU_DIM, V7X_MXU_DIM, 2 * V7X_MXU_DIM), fixed3),
        pl.BlockSpec((2, LRU_WIDTH), fixed2),
        pl.BlockSpec((1, LRU_WIDTH), fixed2),
    ]
    args = [xr, xr, xr, conv_w, conv_b, w_gates, b_gates, lam]
    if reverse:
        in_specs += [pl.BlockSpec((tc, LRU_WIDTH), main), pl.BlockSpec((tc, LRU_WIDTH), main)]
        args += [h_fwd, yr]
    return pl.pallas_call(
        functools.partial(_lru_kernel, reverse, tc, n_chunks),
        grid=(batch, n_chunks),
        in_specs=in_specs,
        out_specs=pl.BlockSpec((tc, LRU_WIDTH), main),
        out_shape=jax.ShapeDtypeStruct((n, LRU_WIDTH), F32),
        scratch_shapes=[
            pltpu.VMEM((tc + 2 * halo, LRU_WIDTH), F32),
            pltpu.VMEM((per_tile, halo, LRU_WIDTH), F32),
            pltpu.VMEM((per_tile, halo, LRU_WIDTH), F32),
            pltpu.VMEM((halo, LRU_WIDTH), F32),
        ],
        compiler_params=_params("parallel", "arbitrary"),
        name="lru_bwd" if reverse else "lru_fwd",
    )(*args)


def _out_kernel(x_ref, attn_ref, lru_ref, ga_ref, gl_ref, wa_ref, wl_ref, gf_ref, wq_ref,
                x1_ref, xnt_ref, qp_ref):
    an = _rms(attn_ref[...], ga_ref[...]).astype(BF16)
    ln = _rms(lru_ref[...], gl_ref[...]).astype(BF16)
    x1 = (x_ref[...] + jnp.dot(an, wa_ref[...], preferred_element_type=F32)
          + jnp.dot(ln, wl_ref[...], preferred_element_type=F32))
    x1_ref[...] = x1
    xn = _rms(x1, gf_ref[...])
    xnt_ref[...] = xn.T.astype(BF16)
    qp_ref[...] = jnp.dot(xn.astype(BF16), wq_ref[...], preferred_element_type=F32)


def _out_proj(x2, attn, lru, g_attn, g_lru, w_out_a, w_out_l, g_ffn, w_query):
    n = x2.shape[0]
    tm = _block(n, 512)
    qw = PEER_HEADS * PEER_KEY_DIM
    row = lambda i: (i, 0)
    fixed = lambda i: (0, 0)
    return pl.pallas_call(
        _out_kernel,
        grid=(n // tm,),
        in_specs=[
            pl.BlockSpec((tm, D_MODEL), row),
            pl.BlockSpec((tm, ATTN_WIDTH), row),
            pl.BlockSpec((tm, LRU_WIDTH), row),
            pl.BlockSpec((1, ATTN_WIDTH), fixed),
            pl.BlockSpec((1, LRU_WIDTH), fixed),
            pl.BlockSpec((ATTN_WIDTH, D_MODEL), fixed),
            pl.BlockSpec((LRU_WIDTH, D_MODEL), fixed),
            pl.BlockSpec((1, D_MODEL), fixed),
            pl.BlockSpec((D_MODEL, qw), fixed),
        ],
        out_specs=[
            pl.BlockSpec((tm, D_MODEL), row),
            pl.BlockSpec((D_MODEL, tm), lambda i: (0, i)),
            pl.BlockSpec((tm, qw), row),
        ],
        out_shape=[
            jax.ShapeDtypeStruct((n, D_MODEL), F32),
            jax.ShapeDtypeStruct((D_MODEL, n), BF16),
            jax.ShapeDtypeStruct((n, qw), F32),
        ],
        compiler_params=_params("parallel"),
        name="out_proj",
    )(x2, attn, lru, g_attn, g_lru, w_out_a, w_out_l, g_ffn, w_query)


def _oddeven_merge(lo, hi, r):
    step = r * 2
    if step < hi - lo:
        yield from _oddeven_merge(lo, hi, step)
        yield from _oddeven_merge(lo + r, hi, step)
        yield from [(i, i + r) for i in range(lo + r, hi - r, step)]
    else:
        yield (lo, lo + r)


def _oddeven_merge_sort(lo, hi):
    if hi - lo >= 1:
        mid = lo + (hi - lo) // 2
        yield from _oddeven_merge_sort(lo, mid)
        yield from _oddeven_merge_sort(mid + 1, hi)
        yield from _oddeven_merge(lo, hi, 1)


def _pop_lists(lists, count):
    lists = list(lists)
    vals = []
    for it in range(count):
        m = jnp.max(lists[0], axis=0, keepdims=True)
        vals.append(m)
        hit = lists[0] == m
        last = min(len(lists), count - it) - 1
        if it < count - 1:
            for c in range(last):
                lists[c] = jnp.where(hit, lists[c + 1], lists[c])
            lists[last] = jnp.where(hit, NEG_INF, lists[last])
    return vals


def _top_values(s, count):
    tiles = [s[v * V7X_SUBLANES:(v + 1) * V7X_SUBLANES, :] for v in range(s.shape[0] // V7X_SUBLANES)]
    for i, j in _oddeven_merge_sort(0, len(tiles) - 1):
        tiles[i], tiles[j] = jnp.maximum(tiles[i], tiles[j]), jnp.minimum(tiles[i], tiles[j])
    return _pop_lists(tiles, count)


def _peer_score_kernel(qp_ref, keys_ref, n1_ref, e1_ref, c2_ref, e2_ref):
    half_tile = V7X_SUBLANES
    sub = lax.broadcasted_iota(jnp.int32, (half_tile, qp_ref.shape[0]), 0)
    for hd in range(PEER_HEADS):
        qh = qp_ref[:, hd * PEER_KEY_DIM:(hd + 1) * PEER_KEY_DIM].astype(BF16)
        nt = (((1,), (1,)), ((), ()))
        s1 = lax.dot_general(keys_ref[hd, 0], qh, nt, preferred_element_type=F32)
        s2 = lax.dot_general(keys_ref[hd, 1], qh, nt, preferred_element_type=F32)
        top1 = _top_values(s1, PEER_TOPK)
        top2 = _top_values(s2, PEER_TOPK)
        a_lo = jnp.concatenate(top1[:half_tile], axis=0)
        a_hi = jnp.concatenate(top1[half_tile:], axis=0)
        lists = []
        for c in range(PEER_TOPK):
            keep = PEER_TOPK // (c + 1)
            row_sums = a_lo + top2[c]
            lists.append(row_sums if keep >= half_tile else jnp.where(sub < keep, row_sums, NEG_INF))
        heads_hi = a_hi + top2[0]
        m12 = top1[0] + top2[0]
        n_lo = jnp.zeros_like(a_lo)
        n_hi = jnp.zeros_like(a_lo)
        z_lo = jnp.zeros_like(a_lo)
        z_hi = jnp.zeros_like(a_lo)
        for it in range(PEER_TOPK):
            m = jnp.maximum(jnp.max(lists[0], axis=0, keepdims=True), jnp.max(heads_hi, axis=0, keepdims=True))
            e = jnp.exp(m - m12)
            hit_lo = lists[0] == m
            hit_hi = heads_hi == m
            n_lo = n_lo + jnp.where(hit_lo, 1.0, 0.0)
            n_hi = n_hi + jnp.where(hit_hi, 1.0, 0.0)
            z_lo = z_lo + jnp.where(hit_lo, e, 0.0)
            z_hi = z_hi + jnp.where(hit_hi, e, 0.0)
            if it < PEER_TOPK - 1:
                last = PEER_TOPK - it - 1
                for c in range(last):
                    lists[c] = jnp.where(hit_lo, lists[c + 1], lists[c])
                lists[last] = jnp.where(hit_lo, NEG_INF, lists[last])
                heads_hi = jnp.where(hit_hi, NEG_INF, heads_hi)
        z = jnp.sum(z_lo + z_hi, axis=0, keepdims=True)
        n1 = jnp.zeros_like(s1)
        rank2 = jnp.zeros_like(s2)
        for r in range(PEER_TOPK):
            n_r = (n_lo if r < half_tile else n_hi)[r % half_tile:r % half_tile + 1, :]
            n1 = jnp.where(s1 == top1[r], n_r, n1)
            rank2 = rank2 + jnp.where(s2 < top2[r], 1.0, 0.0)
        n1_ref[hd] = n1
        e1_ref[hd] = jnp.exp(s1 - top1[0]) / z
        c2_ref[hd] = rank2.astype(BF16)
        e2_ref[hd] = jnp.exp(s2 - top2[0]).astype(BF16)


def _peer_scores(qp, keys_pad):
    n = qp.shape[0]
    tb = _block(n, 256)
    qw = PEER_HEADS * PEER_KEY_DIM
    big = pl.BlockSpec((PEER_HEADS, N_KEYS, tb), lambda i: (0, 0, i))
    shape_f32 = jax.ShapeDtypeStruct((PEER_HEADS, N_KEYS, n), F32)
    shape_bf16 = jax.ShapeDtypeStruct((PEER_HEADS, N_KEYS, n), BF16)
    return pl.pallas_call(
        _peer_score_kernel,
        grid=(n // tb,),
        in_specs=[
            pl.BlockSpec((tb, qw), lambda i: (i, 0)),
            pl.BlockSpec((PEER_HEADS, 2, N_KEYS, PEER_KEY_DIM), lambda i: (0, 0, 0, 0)),
        ],
        out_specs=[big, big, big, big],
        out_shape=[shape_f32, shape_f32, shape_bf16, shape_bf16],
        compiler_params=_params("parallel"),
        name="peer_scores",
    )(qp, keys_pad)


PEER_KEYS_PER_STEP = V7X_SUBLANES
PEER_EXPERTS_PER_STEP = PEER_KEYS_PER_STEP * N_KEYS


def _peer_gate(n1_ref, e1_ref, c2_ref, e2_ref, g_scr):
    tb = g_scr.shape[1]
    bf16_rows = 2 * V7X_SUBLANES

    def key_row(ref, hd, il):
        row = jnp.broadcast_to(ref[hd, il:il + 1, :], (bf16_rows, tb)).astype(BF16)
        return jnp.tile(row, (N_KEYS // bf16_rows, 1))

    for il in range(PEER_KEYS_PER_STEP):
        gate = jnp.zeros((N_KEYS, tb), BF16)
        for hd in range(PEER_HEADS):
            n_row = key_row(n1_ref, hd, il)
            gate = gate + jnp.where(c2_ref[hd] < n_row, e2_ref[hd], 0.0) * key_row(e1_ref, hd, il)
        g_scr[il * N_KEYS:(il + 1) * N_KEYS, :] = gate


PEER_FIRST_MATMULS = 4
PEER_SECOND_SPLIT = 2


def _peer_mix_kernel(xnt_ref, ed_ref, eu_ref, n1_ref, e1_ref, c2_ref, e2_ref, x1_ref, y_ref, g_scr, w_scr):
    ec = pl.program_id(1)

    @pl.when(ec == 0)
    def _():
        y_ref[...] = x1_ref[...]

    _peer_gate(n1_ref, e1_ref, c2_ref, e2_ref, g_scr)
    tn = (((0,), (0,)), ((), ()))
    n_h = PEER_FIRST_MATMULS
    piece = ed_ref.shape[0] // n_h
    pieces = [slice(c * piece, (c + 1) * piece) for c in range(n_h)]
    hids = [jnp.dot(ed_ref[rows, :], xnt_ref[...], preferred_element_type=F32) for rows in pieces]
    parts = []
    for lo, hi in ((0, PEER_SECOND_SPLIT), (PEER_SECOND_SPLIT, n_h)):
        for q in range(lo, hi):
            w_scr[pieces[q], :] = _gelu(hids[q].astype(BF16)) * g_scr[pieces[q], :]
        rows = slice(lo * piece, hi * piece)
        parts.append(lax.dot_general(w_scr[rows, :], eu_ref[rows, :], tn, preferred_element_type=F32))
    y_ref[...] += parts[0] + parts[1]


def _peer_mix(xnt, e_down, e_up, n1, e1, c2, e2, x1):
    n = xnt.shape[1]
    tb = _block(n, 512)
    ec = PEER_EXPERTS_PER_STEP
    tok = lambda t, e: (t, 0)
    exp_ = lambda t, e: (e, 0)
    key1 = pl.BlockSpec((PEER_HEADS, PEER_KEYS_PER_STEP, tb), lambda t, e: (0, e, t))
    key2 = pl.BlockSpec((PEER_HEADS, N_KEYS, tb), lambda t, e: (0, 0, t))
    return pl.pallas_call(
        _peer_mix_kernel,
        grid=(n // tb, N_EXPERTS // ec),
        in_specs=[
            pl.BlockSpec((D_MODEL, tb), lambda t, e: (0, t)),
            pl.BlockSpec((ec, D_MODEL), exp_),
            pl.BlockSpec((ec, D_MODEL), exp_),
            key1, key1, key2, key2,
            pl.BlockSpec((tb, D_MODEL), tok),
        ],
        out_specs=pl.BlockSpec((tb, D_MODEL), tok),
        out_shape=jax.ShapeDtypeStruct((n, D_MODEL), F32),
        scratch_shapes=[pltpu.VMEM((ec, tb), BF16), pltpu.VMEM((ec, tb), BF16)],
        compiler_params=_params("parallel", "arbitrary"),
        name="peer_mix",
    )(xnt, e_down, e_up, n1, e1, c2, e2, x1)


def _rope_tables(seq):
    t = jnp.arange(seq, dtype=jnp.int32)
    row = (t // GRID_W).astype(F32)
    col = (t % GRID_W).astype(F32)
    half = HEAD_DIM // 2
    inv = ROPE_THETA ** (-jnp.arange(0, half, 2, dtype=F32) / half)
    ar = row[:, None] * inv
    ac = col[:, None] * inv
    cos_t = jnp.concatenate([jnp.cos(ar), jnp.cos(ar), jnp.cos(ac), jnp.cos(ac)], axis=-1)
    sin_t = jnp.concatenate([-jnp.sin(ar), jnp.sin(ar), -jnp.sin(ac), jnp.sin(ac)], axis=-1)
    return cos_t, sin_t


def _gate_weights(w_gate_a, w_gate_x):
    per_tile = V7X_MXU_DIM // LRU_BLOCK_W
    n_tiles = LRU_WIDTH // V7X_MXU_DIM

    def tiles(w):
        w = w.reshape(n_tiles, per_tile, LRU_BLOCK_W, LRU_BLOCK_W)
        eye = jnp.eye(per_tile, dtype=w.dtype)
        return jnp.einsum("tpkj,pq->tpkqj", w, eye).reshape(n_tiles, V7X_MXU_DIM, V7X_MXU_DIM)

    return jnp.concatenate([tiles(w_gate_a), tiles(w_gate_x)], axis=-1).astype(BF16)


def _padded_keys(sub_keys):
    z = jnp.zeros_like(sub_keys[:, 0])
    k0 = jnp.concatenate([sub_keys[:, 0], z], axis=-1)
    k1 = jnp.concatenate([z, sub_keys[:, 1]], axis=-1)
    return jnp.stack([k0, k1], axis=1).astype(BF16)


def _layer(x2, p, batch, seq, cos_t, sin_t):
    q, k, v = _qkv_proj(x2, p["g_mix"], p["w_qkv"], p["g_q"], p["g_k"], cos_t, sin_t, seq)
    attn = _attention(q, k, v, batch, seq)
    xr, yr = _xy_proj(x2, p["g_mix"], p["w_xy"])
    h_fwd = _lru_direction(False, xr, p["conv_w"], p["conv_b"], p["w_gates"][0], p["b_gates"][0],
                           p["lam"][0], batch, seq)
    lru = _lru_direction(True, xr, p["conv_w"], p["conv_b"], p["w_gates"][1], p["b_gates"][1],
                         p["lam"][1], batch, seq, h_fwd=h_fwd, yr=yr)
    x1, xnt, qp = _out_proj(x2, attn, lru, p["g_attn_out"], p["g_lru_out"], p["w_out_a"], p["w_out_l"],
                           p["g_ffn"], p["w_query"])
    n1, e1, c2, e2 = _peer_scores(qp, p["keys_pad"])
    return _peer_mix(xnt, p["e_down"], p["e_up"], n1, e1, c2, e2, x1)


def _trunk(x, layers):
    batch, seq, _ = x.shape
    x2 = x.reshape(batch * seq, D_MODEL)
    cos_t, sin_t = _rope_tables(seq)
    for p in layers:
        x2 = _layer(x2, p, batch, seq, cos_t, sin_t)
    return x2.reshape(batch, seq, D_MODEL)


def _layer_params(l, g_mix, w_in, g_q, g_k, conv_w, conv_b, w_gate_a, b_gate_a, w_gate_x, b_gate_x, lru_lambda,
                  g_attn_out, g_lru_out, w_out, g_ffn, w_query, sub_keys, expert_down, expert_up):
    w_in_b = w_in[l].astype(BF16)
    w_out_b = w_out[l].astype(BF16)
    return {
        "g_mix": g_mix[l][None, :],
        "w_qkv": w_in_b[:, :QKV_WIDTH],
        "w_xy": w_in_b[:, QKV_WIDTH:],
        "g_q": g_q[l][None, :],
        "g_k": g_k[l][None, :],
        "conv_w": conv_w[l],
        "conv_b": conv_b[l][None, :],
        "w_gates": [_gate_weights(w_gate_a[l, d], w_gate_x[l, d]) for d in range(2)],
        "b_gates": [jnp.stack([b_gate_a[l, d], b_gate_x[l, d]], axis=0) for d in range(2)],
        "lam": [lru_lambda[l, d][None, :] for d in range(2)],
        "g_attn_out": g_attn_out[l][None, :],
        "g_lru_out": g_lru_out[l][None, :],
        "w_out_a": w_out_b[:ATTN_WIDTH],
        "w_out_l": w_out_b[ATTN_WIDTH:],
        "g_ffn": g_ffn[l][None, :],
        "w_query": w_query[l].astype(BF16),
        "keys_pad": _padded_keys(sub_keys[l]),
        "e_down": expert_down[l].astype(BF16),
        "e_up": expert_up[l].astype(BF16),
    }


def kernel(x_prompt, x_sample, g_mix, w_in, g_q, g_k, conv_w, conv_b, w_gate_a, b_gate_a, w_gate_x, b_gate_x, lru_lambda, g_attn_out, g_lru_out, w_out, g_ffn, w_query, sub_keys, expert_down, expert_up):
    weights = (g_mix, w_in, g_q, g_k, conv_w, conv_b, w_gate_a, b_gate_a, w_gate_x, b_gate_x, lru_lambda,
               g_attn_out, g_lru_out, w_out, g_ffn, w_query, sub_keys, expert_down, expert_up)
    layers = [_layer_params(l, *weights) for l in range(w_in.shape[0])]
    return (_trunk(x_prompt, layers), _trunk(x_sample, layers))
```

```python
import functools
import math

import jax
import jax.numpy as jnp
from jax import lax
from jax.experimental import pallas as pl
from jax.experimental.pallas import tpu as pltpu

F32 = jnp.float32
BF16 = jnp.bfloat16

D_MODEL = 2048
GRID_W = 64
N_HEADS = 8
N_KV_HEADS = 2
HEAD_DIM = 128
HEADS_PER_KV = N_HEADS // N_KV_HEADS
ATTN_WIDTH = N_HEADS * HEAD_DIM
KV_WIDTH = N_KV_HEADS * HEAD_DIM
QKV_WIDTH = ATTN_WIDTH + 2 * KV_WIDTH
ROPE_THETA = 10000.0
LRU_WIDTH = D_MODEL - ATTN_WIDTH
LRU_BLOCK_W = 64
CONV_W = 4
CONV_PAD_LEFT = 2
LRU_C = 8.0
N_KEYS = 128
N_EXPERTS = N_KEYS * N_KEYS
PEER_HEADS = 8
PEER_KEY_DIM = 128
PEER_HALF = PEER_KEY_DIM // 2
PEER_TOPK = 16
EPS = 1e-6

V7X_SUBLANES = 8
V7X_LANES = 128
V7X_MXU_DIM = 256
V7X_VMEM_LIMIT_BYTES = 56 * 1024 * 1024

NEG_INF = float("-inf")


def _params(*semantics, flags=None):
    return pltpu.CompilerParams(dimension_semantics=semantics,
                                vmem_limit_bytes=V7X_VMEM_LIMIT_BYTES,
                                flags=flags)


def _block(n, target):
    b = min(n, target)
    while n % b:
        b //= 2
    return b


def _gelu(x):
    return 0.5 * x * (1.0 + lax.erf(x * (1.0 / math.sqrt(2.0))))


def _sigmoid(x):
    return 0.5 + 0.5 * jnp.tanh(0.5 * x)


def _rms(x, g):
    return x * lax.rsqrt(jnp.mean(x * x, axis=-1, keepdims=True) + EPS) * g


def _qkv_kernel(x_ref, gmix_ref, w_ref, gq_ref, gk_ref, cos_ref, sin_ref,
                q_ref, k_ref, v_ref):
    tm = x_ref.shape[0]
    n_sub = 2 if tm % (4 * V7X_SUBLANES) == 0 else 1
    halves = [slice(c * (tm // n_sub), (c + 1) * (tm // n_sub)) for c in range(n_sub)]
    zs = [jnp.dot(_rms(x_ref[rows, :], gmix_ref[...]).astype(BF16), w_ref[...], preferred_element_type=F32)
          for rows in halves]
    lane = lax.broadcasted_iota(jnp.int32, (tm // n_sub, HEAD_DIM), 1)
    first_half = (lane % (HEAD_DIM // 2)) < (HEAD_DIM // 4)
    scale = HEAD_DIM ** -0.5 * math.log2(math.e)
    gq = gq_ref[...]
    gk = gk_ref[...]
    for rows, z in zip(halves, zs):
        cos = cos_ref[rows, :]
        sin = sin_ref[rows, :]

        def norm_rope(zh, g):
            y = _rms(zh, g)
            partner = jnp.where(first_half,
                                pltpu.roll(y, HEAD_DIM - HEAD_DIM // 4, 1),
                                pltpu.roll(y, HEAD_DIM // 4, 1))
            return y * cos + partner * sin

        for hd in range(N_HEADS):
            sl = slice(hd * HEAD_DIM, (hd + 1) * HEAD_DIM)
            q_ref[rows, sl] = (norm_rope(z[:, sl], gq) * scale).astype(BF16)
        for hd in range(N_KV_HEADS):
            sl = slice(hd * HEAD_DIM, (hd + 1) * HEAD_DIM)
            k_ref[rows, sl] = norm_rope(z[:, ATTN_WIDTH + hd * HEAD_DIM:ATTN_WIDTH + (hd + 1) * HEAD_DIM], gk).astype(BF16)
        ones = jnp.ones((z.shape[0], HEAD_DIM), BF16)
        for hd in range(N_KV_HEADS):
            v0 = ATTN_WIDTH + KV_WIDTH + hd * HEAD_DIM
            v_ref[rows, 2 * hd * HEAD_DIM:(2 * hd + 1) * HEAD_DIM] = z[:, v0:v0 + HEAD_DIM].astype(BF16)
            v_ref[rows, (2 * hd + 1) * HEAD_DIM:(2 * hd + 2) * HEAD_DIM] = ones


def _qkv_proj(x2, g_mix, w_qkv, g_q, g_k, cos_t, sin_t, seq):
    n = x2.shape[0]
    tm = _block(seq, 512)
    nseq_blocks = seq // tm
    row = lambda i: (i, 0)
    fixed = lambda i: (0, 0)
    pos = lambda i: (i % nseq_blocks, 0)
    return pl.pallas_call(
        _qkv_kernel,
        grid=(n // tm,),
        in_specs=[
            pl.BlockSpec((tm, D_MODEL), row),
            pl.BlockSpec((1, D_MODEL), fixed),
            pl.BlockSpec((D_MODEL, QKV_WIDTH), fixed),
            pl.BlockSpec((1, HEAD_DIM), fixed),
            pl.BlockSpec((1, HEAD_DIM), fixed),
            pl.BlockSpec((tm, HEAD_DIM), pos),
            pl.BlockSpec((tm, HEAD_DIM), pos),
        ],
        out_specs=[
            pl.BlockSpec((tm, ATTN_WIDTH), row),
            pl.BlockSpec((tm, KV_WIDTH), row),
            pl.BlockSpec((tm, 2 * KV_WIDTH), row),
        ],
        out_shape=[
            jax.ShapeDtypeStruct((n, ATTN_WIDTH), BF16),
            jax.ShapeDtypeStruct((n, KV_WIDTH), BF16),
            jax.ShapeDtypeStruct((n, 2 * KV_WIDTH), BF16),
        ],
        compiler_params=_params("parallel"),
        name="qkv_proj",
    )(x2, g_mix, w_qkv, g_q, g_k, cos_t, sin_t)


def _xy_kernel(x_ref, gmix_ref, w_ref, xr_ref, yr_ref):
    h = _rms(x_ref[...], gmix_ref[...]).astype(BF16)
    z = jnp.dot(h, w_ref[...], preferred_element_type=F32)
    xr_ref[...] = z[:, :LRU_WIDTH]
    yr_ref[...] = z[:, LRU_WIDTH:]


def _xy_proj(x2, g_mix, w_xy):
    n = x2.shape[0]
    tm = _block(n, 512)
    row = lambda i: (i, 0)
    fixed = lambda i: (0, 0)
    return pl.pallas_call(
        _xy_kernel,
        grid=(n // tm,),
        in_specs=[
            pl.BlockSpec((tm, D_MODEL), row),
            pl.BlockSpec((1, D_MODEL), fixed),
            pl.BlockSpec((D_MODEL, 2 * LRU_WIDTH), fixed),
        ],
        out_specs=[pl.BlockSpec((tm, LRU_WIDTH), row), pl.BlockSpec((tm, LRU_WIDTH), row)],
        out_shape=[jax.ShapeDtypeStruct((n, LRU_WIDTH), F32)] * 2,
        compiler_params=_params("parallel"),
        name="xy_proj",
    )(x2, g_mix, w_xy)


def _attn_kernel(tk, q_ref, k_ref, v_ref, o_ref, m_scr, acc_scr):
    m_scr[...] = jnp.full(m_scr.shape, NEG_INF, F32)
    acc_scr[...] = jnp.zeros(acc_scr.shape, F32)
    tq = q_ref.shape[0]
    lane_tiles = tk // HEAD_DIM

    def kv_block(ki, carry):
        start = pl.multiple_of(ki * tk, tk)
        _attn_block(q_ref, k_ref[pl.ds(start, tk), :], v_ref[pl.ds(start, tk), :], m_scr, acc_scr, tq, tk, lane_tiles)
        return carry

    lax.fori_loop(0, k_ref.shape[0] // tk, kv_block, 0)
    for hd in range(HEADS_PER_KV):
        acc = acc_scr[hd]
        o_ref[:, hd * HEAD_DIM:(hd + 1) * HEAD_DIM] = acc[:, :HEAD_DIM] / acc[:, HEAD_DIM:]


def _attn_block(q_ref, k, v, m_scr, acc_scr, tq, tk, lane_tiles):
    nt = (((1,), (1,)), ((), ()))
    for hd in range(HEADS_PER_KV):
        q = q_ref[:, hd * HEAD_DIM:(hd + 1) * HEAD_DIM]
        if hd == 0 and tk % (2 * V7X_MXU_DIM) == 0:
            s = jnp.concatenate([lax.dot_general(q, k[:tk // 2], nt, preferred_element_type=F32),
                                 lax.dot_general(q, k[tk // 2:], nt, preferred_element_type=F32)], axis=1)
        else:
            s = lax.dot_general(q, k, nt, preferred_element_type=F32)
        m_prev = m_scr[hd]
        m_next = jnp.maximum(m_prev, jnp.max(s, axis=1, keepdims=True))
        p = jnp.exp2(s - jnp.tile(m_next, (1, lane_tiles))).astype(BF16)
        alpha = jnp.exp2(m_prev - m_next)
        if hd == HEADS_PER_KV - 1 and tq % 32 == 0:
            pv = jnp.concatenate([jnp.dot(p[:tq // 2], v, preferred_element_type=F32),
                                  jnp.dot(p[tq // 2:], v, preferred_element_type=F32)], axis=0)
        else:
            pv = jnp.dot(p, v, preferred_element_type=F32)
        acc_scr[hd] = jnp.tile(alpha, (1, 2)) * acc_scr[hd] + pv
        m_scr[hd] = m_next


def _attention(q, k, v, batch, seq):
    n = q.shape[0]
    tq = _block(seq, 1024)
    tk = _block(seq, 2048)
    nq = seq // tq
    group_w = HEADS_PER_KV * HEAD_DIM
    return pl.pallas_call(
        functools.partial(_attn_kernel, tk),
        grid=(batch, N_KV_HEADS, nq),
        in_specs=[
            pl.BlockSpec((tq, group_w), lambda b, g, qi: (b * nq + qi, g)),
            pl.BlockSpec((seq, HEAD_DIM), lambda b, g, qi: (b, g)),
            pl.BlockSpec((seq, 2 * HEAD_DIM), lambda b, g, qi: (b, g)),
        ],
        out_specs=pl.BlockSpec((tq, group_w), lambda b, g, qi: (b * nq + qi, g)),
        out_shape=jax.ShapeDtypeStruct((n, ATTN_WIDTH), F32),
        scratch_shapes=[
            pltpu.VMEM((HEADS_PER_KV, tq, HEAD_DIM), F32),
            pltpu.VMEM((HEADS_PER_KV, tq, 2 * HEAD_DIM), F32),
        ],
        compiler_params=_params("parallel", "parallel", "parallel"),
        name="flash_attention",
    )(q, k, v)


def _lru_kernel(reverse, tc, n_chunks, *refs):
    if reverse:
        (xm_ref, xp_ref, xn_ref, cw_ref, cb_ref, wg_ref, bg_ref, lam_ref, hf_ref, yr_ref,
         out_ref, ext_scr, a_scr, u_scr, carry_scr) = refs
    else:
        (xm_ref, xp_ref, xn_ref, cw_ref, cb_ref, wg_ref, bg_ref, lam_ref,
         out_ref, ext_scr, a_scr, u_scr, carry_scr) = refs
    c = pl.program_id(1)
    chunk = (n_chunks - 1 - c) if reverse else c
    halo = V7X_SUBLANES

    @pl.when(c == 0)
    def _():
        carry_scr[...] = jnp.zeros(carry_scr.shape, F32)

    ext_scr[0:halo, :] = jnp.where(chunk == 0, 0.0, xp_ref[...])
    ext_scr[halo:halo + tc, :] = xm_ref[...]
    ext_scr[halo + tc:2 * halo + tc, :] = jnp.where(chunk == n_chunks - 1, 0.0, xn_ref[...])
    xc = cb_ref[...]
    for tap in range(CONV_W):
        start = halo + tap - CONV_PAD_LEFT
        xc = xc + ext_scr[start:start + tc, :] * cw_ref[tap:tap + 1, :]

    xcb = xc.astype(BF16)
    za, zx = [], []
    for g in range(LRU_WIDTH // V7X_MXU_DIM):
        zg = jnp.dot(xcb[:, g * V7X_MXU_DIM:(g + 1) * V7X_MXU_DIM], wg_ref[g],
                     preferred_element_type=F32)
        za.append(zg[:, :V7X_MXU_DIM])
        zx.append(zg[:, V7X_MXU_DIM:])
    r = _sigmoid(jnp.concatenate(za, axis=1) + bg_ref[0:1, :])
    i = _sigmoid(jnp.concatenate(zx, axis=1) + bg_ref[1:2, :])
    lam = lam_ref[...]
    softplus_neg = jnp.maximum(-lam, 0.0) + jnp.log(1.0 + jnp.exp(-jnp.abs(lam)))
    a = jnp.exp((-LRU_C * softplus_neg) * r)
    u = jnp.sqrt(1.0 - a * a) * (i * xc)

    nt = tc // V7X_SUBLANES
    a3 = a.reshape(nt, V7X_SUBLANES, LRU_WIDTH)
    u3 = u.reshape(nt, V7X_SUBLANES, LRU_WIDTH)
    row = lax.broadcasted_iota(jnp.int32, a3.shape, 1)
    for s in (1, 2, 4):
        if reverse:
            shift, valid = V7X_SUBLANES - s, row < V7X_SUBLANES - s
        else:
            shift, valid = s, row >= s
        a_sh = pltpu.roll(a3, shift, 1)
        u_sh = pltpu.roll(u3, shift, 1)
        u3 = u3 + jnp.where(valid, a3 * u_sh, 0.0)
        a3 = jnp.where(valid, a3 * a_sh, a3)
    a_scr[...] = a3
    u_scr[...] = u3

    def tile_step(t, carry):
        tt = (nt - 1 - t) if reverse else t
        h = u_scr[tt] + a_scr[tt] * carry
        u_scr[tt] = h
        edge = h[0:1, :] if reverse else h[V7X_SUBLANES - 1:V7X_SUBLANES, :]
        return jnp.broadcast_to(edge, h.shape)

    carry_scr[...] = lax.fori_loop(0, nt, tile_step, carry_scr[...])
    h_all = u_scr[...].reshape(tc, LRU_WIDTH)
    if reverse:
        out_ref[...] = (h_all + hf_ref[...]) * _gelu(yr_ref[...])
    else:
        out_ref[...] = h_all


def _lru_direction(reverse, xr, conv_w, conv_b, w_gates, b_gates, lam, batch, seq, h_fwd=None, yr=None):
    n = xr.shape[0]
    tc = _block(seq, 256)
    n_chunks = seq // tc
    halo = V7X_SUBLANES
    per_tile = tc // halo
    n_tiles = n // halo

    def chunk_of(c):
        return (n_chunks - 1 - c) if reverse else c

    main = lambda b, c: (b * n_chunks + chunk_of(c), 0)
    prev = lambda b, c: (jnp.maximum((b * n_chunks + chunk_of(c)) * per_tile - 1, 0), 0)
    nxt = lambda b, c: (jnp.minimum((b * n_chunks + chunk_of(c) + 1) * per_tile, n_tiles - 1), 0)
    fixed2 = lambda b, c: (0, 0)
    fixed3 = lambda b, c: (0, 0, 0)
    in_specs = [
        pl.BlockSpec((tc, LRU_WIDTH), main),
        pl.BlockSpec((halo, LRU_WIDTH), prev),
        pl.BlockSpec((halo, LRU_WIDTH), nxt),
        pl.BlockSpec((CONV_W, LRU_WIDTH), fixed2),
        pl.BlockSpec((1, LRU_WIDTH), fixed2),
        pl.BlockSpec((LRU_WIDTH // V7X_MXU_DIM, V7X_MXU_DIM, 2 * V7X_MXU_DIM), fixed3),
        pl.BlockSpec((2, LRU_WIDTH), fixed2),
        pl.BlockSpec((1, LRU_WIDTH), fixed2),
    ]
    args = [xr, xr, xr, conv_w, conv_b, w_gates, b_gates, lam]
    if reverse:
        in_specs += [pl.BlockSpec((tc, LRU_WIDTH), main), pl.BlockSpec((tc, LRU_WIDTH), main)]
        args += [h_fwd, yr]
    return pl.pallas_call(
        functools.partial(_lru_kernel, reverse, tc, n_chunks),
        grid=(batch, n_chunks),
        in_specs=in_specs,
        out_specs=pl.BlockSpec((tc, LRU_WIDTH), main),
        out_shape=jax.ShapeDtypeStruct((n, LRU_WIDTH), F32),
        scratch_shapes=[
            pltpu.VMEM((tc + 2 * halo, LRU_WIDTH), F32),
            pltpu.VMEM((per_tile, halo, LRU_WIDTH), F32),
            pltpu.VMEM((per_tile, halo, LRU_WIDTH), F32),
            pltpu.VMEM((halo, LRU_WIDTH), F32),
        ],
        compiler_params=_params("parallel", "arbitrary"),
        name="lru_bwd" if reverse else "lru_fwd",
    )(*args)


def _out_kernel(x_ref, attn_ref, lru_ref, ga_ref, gl_ref, wa_ref, wl_ref, gf_ref, wq_ref,
                x1_ref, xnt_ref, qp_ref):
    an = _rms(attn_ref[...], ga_ref[...]).astype(BF16)
    ln = _rms(lru_ref[...], gl_ref[...]).astype(BF16)
    x1 = (x_ref[...] + jnp.dot(an, wa_ref[...], preferred_element_type=F32)
          + jnp.dot(ln, wl_ref[...], preferred_element_type=F32))
    x1_ref[...] = x1
    xn = _rms(x1, gf_ref[...])
    xnt_ref[...] = xn.T.astype(BF16)
    qp_ref[...] = jnp.dot(xn.astype(BF16), wq_ref[...], preferred_element_type=F32)


def _out_proj(x2, attn, lru, g_attn, g_lru, w_out_a, w_out_l, g_ffn, w_query):
    n = x2.shape[0]
    tm = _block(n, 512)
    qw = PEER_HEADS * PEER_KEY_DIM
    row = lambda i: (i, 0)
    fixed = lambda i: (0, 0)
    return pl.pallas_call(
        _out_kernel,
        grid=(n // tm,),
        in_specs=[
            pl.BlockSpec((tm, D_MODEL), row),
            pl.BlockSpec((tm, ATTN_WIDTH), row),
            pl.BlockSpec((tm, LRU_WIDTH), row),
            pl.BlockSpec((1, ATTN_WIDTH), fixed),
            pl.BlockSpec((1, LRU_WIDTH), fixed),
            pl.BlockSpec((ATTN_WIDTH, D_MODEL), fixed),
            pl.BlockSpec((LRU_WIDTH, D_MODEL), fixed),
            pl.BlockSpec((1, D_MODEL), fixed),
            pl.BlockSpec((D_MODEL, qw), fixed),
        ],
        out_specs=[
            pl.BlockSpec((tm, D_MODEL), row),
            pl.BlockSpec((D_MODEL, tm), lambda i: (0, i)),
            pl.BlockSpec((tm, qw), row),
        ],
        out_shape=[
            jax.ShapeDtypeStruct((n, D_MODEL), F32),
            jax.ShapeDtypeStruct((D_MODEL, n), BF16),
            jax.ShapeDtypeStruct((n, qw), F32),
        ],
        compiler_params=_params("parallel"),
        name="out_proj",
    )(x2, attn, lru, g_attn, g_lru, w_out_a, w_out_l, g_ffn, w_query)


def _oddeven_merge(lo, hi, r):
    step = r * 2
    if step < hi - lo:
        yield from _oddeven_merge(lo, hi, step)
        yield from _oddeven_merge(lo + r, hi, step)
        yield from [(i, i + r) for i in range(lo + r, hi - r, step)]
    else:
        yield (lo, lo + r)


def _oddeven_merge_sort(lo, hi):
    if hi - lo >= 1:
        mid = lo + (hi - lo) // 2
        yield from _oddeven_merge_sort(lo, mid)
        yield from _oddeven_merge_sort(mid + 1, hi)
        yield from _oddeven_merge(lo, hi, 1)


def _pop_lists(lists, count):
    lists = list(lists)
    vals = []
    for it in range(count):
        m = jnp.max(lists[0], axis=0, keepdims=True)
        vals.append(m)
        hit = lists[0] == m
        last = min(len(lists), count - it) - 1
        if it < count - 1:
            for c in range(last):
                lists[c] = jnp.where(hit, lists[c + 1], lists[c])
            lists[last] = jnp.where(hit, NEG_INF, lists[last])
    return vals


def _top_values(s, count):
    tiles = [s[v * V7X_SUBLANES:(v + 1) * V7X_SUBLANES, :] for v in range(s.shape[0] // V7X_SUBLANES)]
    for i, j in _oddeven_merge_sort(0, len(tiles) - 1):
        tiles[i], tiles[j] = jnp.maximum(tiles[i], tiles[j]), jnp.minimum(tiles[i], tiles[j])
    return _pop_lists(tiles, count)


def _peer_score_kernel(qp_ref, keys_ref, n1_ref, e1_ref, c2_ref, e2_ref):
    half_tile = V7X_SUBLANES
    sub = lax.broadcasted_iota(jnp.int32, (half_tile, qp_ref.shape[0]), 0)
    for hd in range(PEER_HEADS):
        qh = qp_ref[:, hd * PEER_KEY_DIM:(hd + 1) * PEER_KEY_DIM].astype(BF16)
        nt = (((1,), (1,)), ((), ()))
        s1 = lax.dot_general(keys_ref[hd, 0], qh, nt, preferred_element_type=F32)
        s2 = lax.dot_general(keys_ref[hd, 1], qh, nt, preferred_element_type=F32)
        top1 = _top_values(s1, PEER_TOPK)
        top2 = _top_values(s2, PEER_TOPK)
        a_lo = jnp.concatenate(top1[:half_tile], axis=0)
        a_hi = jnp.concatenate(top1[half_tile:], axis=0)
        lists = []
        for c in range(PEER_TOPK):
            keep = PEER_TOPK // (c + 1)
            row_sums = a_lo + top2[c]
            lists.append(row_sums if keep >= half_tile else jnp.where(sub < keep, row_sums, NEG_INF))
        heads_hi = a_hi + top2[0]
        m12 = top1[0] + top2[0]
        n_lo = jnp.zeros_like(a_lo)
        n_hi = jnp.zeros_like(a_lo)
        z_lo = jnp.zeros_like(a_lo)
        z_hi = jnp.zeros_like(a_lo)
        for it in range(PEER_TOPK):
            m = jnp.maximum(jnp.max(lists[0], axis=0, keepdims=True), jnp.max(heads_hi, axis=0, keepdims=True))
            e = jnp.exp(m - m12)
            hit_lo = lists[0] == m
            hit_hi = heads_hi == m
            n_lo = n_lo + jnp.where(hit_lo, 1.0, 0.0)
            n_hi = n_hi + jnp.where(hit_hi, 1.0, 0.0)
            z_lo = z_lo + jnp.where(hit_lo, e, 0.0)
            z_hi = z_hi + jnp.where(hit_hi, e, 0.0)
            if it < PEER_TOPK - 1:
                last = PEER_TOPK - it - 1
                for c in range(last):
                    lists[c] = jnp.where(hit_lo, lists[c + 1], lists[c])
                lists[last] = jnp.where(hit_lo, NEG_INF, lists[last])
                heads_hi = jnp.where(hit_hi, NEG_INF, heads_hi)
        z = jnp.sum(z_lo + z_hi, axis=0, keepdims=True)
        n1 = jnp.zeros_like(s1)
        rank2 = jnp.zeros_like(s2)
        for r in range(PEER_TOPK):
            n_r = (n_lo if r < half_tile else n_hi)[r % half_tile:r % half_tile + 1, :]
            n1 = jnp.where(s1 == top1[r], n_r, n1)
            rank2 = rank2 + jnp.where(s2 < top2[r], 1.0, 0.0)
        n1_ref[hd] = n1
        e1_ref[hd] = jnp.exp(s1 - top1[0]) / z
        c2_ref[hd] = rank2.astype(BF16)
        e2_ref[hd] = jnp.exp(s2 - top2[0]).astype(BF16)


def _peer_scores(qp, keys_pad):
    n = qp.shape[0]
    tb = _block(n, 256)
    qw = PEER_HEADS * PEER_KEY_DIM
    big = pl.BlockSpec((PEER_HEADS, N_KEYS, tb), lambda i: (0, 0, i))
    shape_f32 = jax.ShapeDtypeStruct((PEER_HEADS, N_KEYS, n), F32)
    shape_bf16 = jax.ShapeDtypeStruct((PEER_HEADS, N_KEYS, n), BF16)
    return pl.pallas_call(
        _peer_score_kernel,
        grid=(n // tb,),
        in_specs=[
            pl.BlockSpec((tb, qw), lambda i: (i, 0)),
            pl.BlockSpec((PEER_HEADS, 2, N_KEYS, PEER_KEY_DIM), lambda i: (0, 0, 0, 0)),
        ],
        out_specs=[big, big, big, big],
        out_shape=[shape_f32, shape_f32, shape_bf16, shape_bf16],
        compiler_params=_params("parallel"),
        name="peer_scores",
    )(qp, keys_pad)


PEER_KEYS_PER_STEP = V7X_SUBLANES
PEER_EXPERTS_PER_STEP = PEER_KEYS_PER_STEP * N_KEYS


def _peer_gate(n1_ref, e1_ref, c2_ref, e2_ref, g_scr):
    tb = g_scr.shape[1]
    bf16_rows = 2 * V7X_SUBLANES

    def key_row(ref, hd, il):
        row = jnp.broadcast_to(ref[hd, il:il + 1, :], (bf16_rows, tb)).astype(BF16)
        return jnp.tile(row, (N_KEYS // bf16_rows, 1))

    for il in range(PEER_KEYS_PER_STEP):
        gate = jnp.zeros((N_KEYS, tb), BF16)
        for hd in range(PEER_HEADS):
            n_row = key_row(n1_ref, hd, il)
            gate = gate + jnp.where(c2_ref[hd] < n_row, e2_ref[hd], 0.0) * key_row(e1_ref, hd, il)
        g_scr[il * N_KEYS:(il + 1) * N_KEYS, :] = gate


PEER_FIRST_MATMULS = 4
PEER_SECOND_SPLIT = 2


def _peer_mix_kernel(xnt_ref, ed_ref, eu_ref, n1_ref, e1_ref, c2_ref, e2_ref, x1_ref, y_ref, g_scr, w_scr):
    ec = pl.program_id(1)

    @pl.when(ec == 0)
    def _():
        y_ref[...] = x1_ref[...]

    _peer_gate(n1_ref, e1_ref, c2_ref, e2_ref, g_scr)
    tn = (((0,), (0,)), ((), ()))
    n_h = PEER_FIRST_MATMULS
    piece = ed_ref.shape[0] // n_h
    pieces = [slice(c * piece, (c + 1) * piece) for c in range(n_h)]
    hids = [jnp.dot(ed_ref[rows, :], xnt_ref[...], preferred_element_type=F32) for rows in pieces]
    parts = []
    for lo, hi in ((0, PEER_SECOND_SPLIT), (PEER_SECOND_SPLIT, n_h)):
        for q in range(lo, hi):
            w_scr[pieces[q], :] = _gelu(hids[q].astype(BF16)) * g_scr[pieces[q], :]
        rows = slice(lo * piece, hi * piece)
        parts.append(lax.dot_general(w_scr[rows, :], eu_ref[rows, :], tn, preferred_element_type=F32))
    y_ref[...] += parts[0] + parts[1]


def _peer_mix(xnt, e_down, e_up, n1, e1, c2, e2, x1):
    n = xnt.shape[1]
    tb = _block(n, 512)
    ec = PEER_EXPERTS_PER_STEP
    tok = lambda t, e: (t, 0)
    exp_ = lambda t, e: (e, 0)
    key1 = pl.BlockSpec((PEER_HEADS, PEER_KEYS_PER_STEP, tb), lambda t, e: (0, e, t))
    key2 = pl.BlockSpec((PEER_HEADS, N_KEYS, tb), lambda t, e: (0, 0, t))
    return pl.pallas_call(
        _peer_mix_kernel,
        grid=(n // tb, N_EXPERTS // ec),
        in_specs=[
            pl.BlockSpec((D_MODEL, tb), lambda t, e: (0, t)),
            pl.BlockSpec((ec, D_MODEL), exp_),
            pl.BlockSpec((ec, D_MODEL), exp_),
            key1, key1, key2, key2,
            pl.BlockSpec((tb, D_MODEL), tok),
        ],
        out_specs=pl.BlockSpec((tb, D_MODEL), tok),
        out_shape=jax.ShapeDtypeStruct((n, D_MODEL), F32),
        scratch_shapes=[pltpu.VMEM((ec, tb), BF16), pltpu.VMEM((ec, tb), BF16)],
        compiler_params=_params("parallel", "arbitrary"),
        name="peer_mix",
    )(xnt, e_down, e_up, n1, e1, c2, e2, x1)


def _rope_tables(seq):
    t = jnp.arange(seq, dtype=jnp.int32)
    row = (t // GRID_W).astype(F32)
    col = (t % GRID_W).astype(F32)
    half = HEAD_DIM // 2
    inv = ROPE_THETA ** (-jnp.arange(0, half, 2, dtype=F32) / half)
    ar = row[:, None] * inv
    ac = col[:, None] * inv
    cos_t = jnp.concatenate([jnp.cos(ar), jnp.cos(ar), jnp.cos(ac), jnp.cos(ac)], axis=-1)
    sin_t = jnp.concatenate([-jnp.sin(ar), jnp.sin(ar), -jnp.sin(ac), jnp.sin(ac)], axis=-1)
    return cos_t, sin_t


def _gate_weights(w_gate_a, w_gate_x):
    per_tile = V7X_MXU_DIM // LRU_BLOCK_W
    n_tiles = LRU_WIDTH // V7X_MXU_DIM

    def tiles(w):
        w = w.reshape(n_tiles, per_tile, LRU_BLOCK_W, LRU_BLOCK_W)
        eye = jnp.eye(per_tile, dtype=w.dtype)
        return jnp.einsum("tpkj,pq->tpkqj", w, eye).reshape(n_tiles, V7X_MXU_DIM, V7X_MXU_DIM)

    return jnp.concatenate([tiles(w_gate_a), tiles(w_gate_x)], axis=-1).astype(BF16)


def _padded_keys(sub_keys):
    z = jnp.zeros_like(sub_keys[:, 0])
    k0 = jnp.concatenate([sub_keys[:, 0], z], axis=-1)
    k1 = jnp.concatenate([z, sub_keys[:, 1]], axis=-1)
    return jnp.stack([k0, k1], axis=1).astype(BF16)


def _layer(x2, p, batch, seq, cos_t, sin_t):
    q, k, v = _qkv_proj(x2, p["g_mix"], p["w_qkv"], p["g_q"], p["g_k"], cos_t, sin_t, seq)
    attn = _attention(q, k, v, batch, seq)
    xr, yr = _xy_proj(x2, p["g_mix"], p["w_xy"])
    h_fwd = _lru_direction(False, xr, p["conv_w"], p["conv_b"], p["w_gates"][0], p["b_gates"][0],
                           p["lam"][0], batch, seq)
    lru = _lru_direction(True, xr, p["conv_w"], p["conv_b"], p["w_gates"][1], p["b_gates"][1],
                         p["lam"][1], batch, seq, h_fwd=h_fwd, yr=yr)
    x1, xnt, qp = _out_proj(x2, attn, lru, p["g_attn_out"], p["g_lru_out"], p["w_out_a"], p["w_out_l"],
                           p["g_ffn"], p["w_query"])
    n1, e1, c2, e2 = _peer_scores(qp, p["keys_pad"])
    return _peer_mix(xnt, p["e_down"], p["e_up"], n1, e1, c2, e2, x1)


def _trunk(x, layers):
    batch, seq, _ = x.shape
    x2 = x.reshape(batch * seq, D_MODEL)
    cos_t, sin_t = _rope_tables(seq)
    for p in layers:
        x2 = _layer(x2, p, batch, seq, cos_t, sin_t)
    return x2.reshape(batch, seq, D_MODEL)


def _layer_params(l, g_mix, w_in, g_q, g_k, conv_w, conv_b, w_gate_a, b_gate_a, w_gate_x, b_gate_x, lru_lambda,
                  g_attn_out, g_lru_out, w_out, g_ffn, w_query, sub_keys, expert_down, expert_up):
    w_in_b = w_in[l].astype(BF16)
    w_out_b = w_out[l].astype(BF16)
    return {
        "g_mix": g_mix[l][None, :],
        "w_qkv": w_in_b[:, :QKV_WIDTH],
        "w_xy": w_in_b[:, QKV_WIDTH:],
        "g_q": g_q[l][None, :],
        "g_k": g_k[l][None, :],
        "conv_w": conv_w[l],
        "conv_b": conv_b[l][None, :],
        "w_gates": [_gate_weights(w_gate_a[l, d], w_gate_x[l, d]) for d in range(2)],
        "b_gates": [jnp.stack([b_gate_a[l, d], b_gate_x[l, d]], axis=0) for d in range(2)],
        "lam": [lru_lambda[l, d][None, :] for d in range(2)],
        "g_attn_out": g_attn_out[l][None, :],
        "g_lru_out": g_lru_out[l][None, :],
        "w_out_a": w_out_b[:ATTN_WIDTH],
        "w_out_l": w_out_b[ATTN_WIDTH:],
        "g_ffn": g_ffn[l][None, :],
        "w_query": w_query[l].astype(BF16),
        "keys_pad": _padded_keys(sub_keys[l]),
        "e_down": expert_down[l].astype(BF16),
        "e_up": expert_up[l].astype(BF16),
    }


def kernel(x_prompt, x_sample, g_mix, w_in, g_q, g_k, conv_w, conv_b, w_gate_a, b_gate_a, w_gate_x, b_gate_x, lru_lambda, g_attn_out, g_lru_out, w_out, g_ffn, w_query, sub_keys, expert_down, expert_up):
    weights = (g_mix, w_in, g_q, g_k, conv_w, conv_b, w_gate_a, b_gate_a, w_gate_x, b_gate_x, lru_lambda,
               g_attn_out, g_lru_out, w_out, g_ffn, w_query, sub_keys, expert_down, expert_up)
    layers = [_layer_params(l, *weights) for l in range(w_in.shape[0])]
    return (_trunk(x_prompt, layers), _trunk(x_sample, layers))
```

```python
import functools
import math

import jax
import jax.numpy as jnp
from jax import lax
from jax.experimental import pallas as pl
from jax.experimental.pallas import tpu as pltpu

F32 = jnp.float32
BF16 = jnp.bfloat16

D_MODEL = 2048
GRID_W = 64
N_HEADS = 8
N_KV_HEADS = 2
HEAD_DIM = 128
HEADS_PER_KV = N_HEADS // N_KV_HEADS
ATTN_WIDTH = N_HEADS * HEAD_DIM
KV_WIDTH = N_KV_HEADS * HEAD_DIM
QKV_WIDTH = ATTN_WIDTH + 2 * KV_WIDTH
ROPE_THETA = 10000.0
LRU_WIDTH = D_MODEL - ATTN_WIDTH
LRU_BLOCK_W = 64
CONV_W = 4
CONV_PAD_LEFT = 2
LRU_C = 8.0
N_KEYS = 128
N_EXPERTS = N_KEYS * N_KEYS
PEER_HEADS = 8
PEER_KEY_DIM = 128
PEER_HALF = PEER_KEY_DIM // 2
PEER_TOPK = 16
EPS = 1e-6

V7X_SUBLANES = 8
V7X_LANES = 128
V7X_MXU_DIM = 256
V7X_VMEM_LIMIT_BYTES = 56 * 1024 * 1024

NEG_INF = float("-inf")


def _params(*semantics, flags=None):
    return pltpu.CompilerParams(dimension_semantics=semantics,
                                vmem_limit_bytes=V7X_VMEM_LIMIT_BYTES,
                                flags=flags)


def _block(n, target):
    b = min(n, target)
    while n % b:
        b //= 2
    return b


def _gelu(x):
    return 0.5 * x * (1.0 + lax.erf(x * (1.0 / math.sqrt(2.0))))


def _sigmoid(x):
    return 0.5 + 0.5 * jnp.tanh(0.5 * x)


def _rms(x, g):
    return x * lax.rsqrt(jnp.mean(x * x, axis=-1, keepdims=True) + EPS) * g


def _qkv_kernel(x_ref, gmix_ref, w_ref, gq_ref, gk_ref, cos_ref, sin_ref,
                q_ref, k_ref, v_ref):
    tm = x_ref.shape[0]
    n_sub = 2 if tm % (4 * V7X_SUBLANES) == 0 else 1
    halves = [slice(c * (tm // n_sub), (c + 1) * (tm // n_sub)) for c in range(n_sub)]
    zs = [jnp.dot(_rms(x_ref[rows, :], gmix_ref[...]).astype(BF16), w_ref[...], preferred_element_type=F32)
          for rows in halves]
    lane = lax.broadcasted_iota(jnp.int32, (tm // n_sub, HEAD_DIM), 1)
    first_half = (lane % (HEAD_DIM // 2)) < (HEAD_DIM // 4)
    scale = HEAD_DIM ** -0.5 * math.log2(math.e)
    gq = gq_ref[...]
    gk = gk_ref[...]
    for rows, z in zip(halves, zs):
        cos = cos_ref[rows, :]
        sin = sin_ref[rows, :]

        def norm_rope(zh, g):
            y = _rms(zh, g)
            partner = jnp.where(first_half,
                                pltpu.roll(y, HEAD_DIM - HEAD_DIM // 4, 1),
                                pltpu.roll(y, HEAD_DIM // 4, 1))
            return y * cos + partner * sin

        for hd in range(N_HEADS):
            sl = slice(hd * HEAD_DIM, (hd + 1) * HEAD_DIM)
            q_ref[rows, sl] = (norm_rope(z[:, sl], gq) * scale).astype(BF16)
        for hd in range(N_KV_HEADS):
            sl = slice(hd * HEAD_DIM, (hd + 1) * HEAD_DIM)
            k_ref[rows, sl] = norm_rope(z[:, ATTN_WIDTH + hd * HEAD_DIM:ATTN_WIDTH + (hd + 1) * HEAD_DIM], gk).astype(BF16)
        ones = jnp.ones((z.shape[0], HEAD_DIM), BF16)
        for hd in range(N_KV_HEADS):
            v0 = ATTN_WIDTH + KV_WIDTH + hd * HEAD_DIM
            v_ref[rows, 2 * hd * HEAD_DIM:(2 * hd + 1) * HEAD_DIM] = z[:, v0:v0 + HEAD_DIM].astype(BF16)
            v_ref[rows, (2 * hd + 1) * HEAD_DIM:(2 * hd + 2) * HEAD_DIM] = ones


def _qkv_proj(x2, g_mix, w_qkv, g_q, g_k, cos_t, sin_t, seq):
    n = x2.shape[0]
    tm = _block(seq, 512)
    nseq_blocks = seq // tm
    row = lambda i: (i, 0)
    fixed = lambda i: (0, 0)
    pos = lambda i: (i % nseq_blocks, 0)
    return pl.pallas_call(
        _qkv_kernel,
        grid=(n // tm,),
        in_specs=[
            pl.BlockSpec((tm, D_MODEL), row),
            pl.BlockSpec((1, D_MODEL), fixed),
            pl.BlockSpec((D_MODEL, QKV_WIDTH), fixed),
            pl.BlockSpec((1, HEAD_DIM), fixed),
            pl.BlockSpec((1, HEAD_DIM), fixed),
            pl.BlockSpec((tm, HEAD_DIM), pos),
            pl.BlockSpec((tm, HEAD_DIM), pos),
        ],
        out_specs=[
            pl.BlockSpec((tm, ATTN_WIDTH), row),
            pl.BlockSpec((tm, KV_WIDTH), row),
            pl.BlockSpec((tm, 2 * KV_WIDTH), row),
        ],
        out_shape=[
            jax.ShapeDtypeStruct((n, ATTN_WIDTH), BF16),
            jax.ShapeDtypeStruct((n, KV_WIDTH), BF16),
            jax.ShapeDtypeStruct((n, 2 * KV_WIDTH), BF16),
        ],
        compiler_params=_params("parallel"),
        name="qkv_proj",
    )(x2, g_mix, w_qkv, g_q, g_k, cos_t, sin_t)


def _xy_kernel(x_ref, gmix_ref, w_ref, xr_ref, yr_ref):
    h = _rms(x_ref[...], gmix_ref[...]).astype(BF16)
    z = jnp.dot(h, w_ref[...], preferred_element_type=F32)
    xr_ref[...] = z[:, :LRU_WIDTH]
    yr_ref[...] = z[:, LRU_WIDTH:]


def _xy_proj(x2, g_mix, w_xy):
    n = x2.shape[0]
    tm = _block(n, 512)
    row = lambda i: (i, 0)
    fixed = lambda i: (0, 0)
    return pl.pallas_call(
        _xy_kernel,
        grid=(n // tm,),
        in_specs=[
            pl.BlockSpec((tm, D_MODEL), row),
            pl.BlockSpec((1, D_MODEL), fixed),
            pl.BlockSpec((D_MODEL, 2 * LRU_WIDTH), fixed),
        ],
        out_specs=[pl.BlockSpec((tm, LRU_WIDTH), row), pl.BlockSpec((tm, LRU_WIDTH), row)],
        out_shape=[jax.ShapeDtypeStruct((n, LRU_WIDTH), F32)] * 2,
        compiler_params=_params("parallel"),
        name="xy_proj",
    )(x2, g_mix, w_xy)


def _attn_kernel(tk, q_ref, k_ref, v_ref, o_ref, m_scr, acc_scr):
    m_scr[...] = jnp.full(m_scr.shape, NEG_INF, F32)
    acc_scr[...] = jnp.zeros(acc_scr.shape, F32)
    tq = q_ref.shape[0]
    lane_tiles = tk // HEAD_DIM

    def kv_block(ki, carry):
        start = pl.multiple_of(ki * tk, tk)
        _attn_block(q_ref, k_ref[pl.ds(start, tk), :], v_ref[pl.ds(start, tk), :], m_scr, acc_scr, tq, tk, lane_tiles)
        return carry

    lax.fori_loop(0, k_ref.shape[0] // tk, kv_block, 0)
    for hd in range(HEADS_PER_KV):
        acc = acc_scr[hd]
        o_ref[:, hd * HEAD_DIM:(hd + 1) * HEAD_DIM] = acc[:, :HEAD_DIM] / acc[:, HEAD_DIM:]


def _attn_block(q_ref, k, v, m_scr, acc_scr, tq, tk, lane_tiles):
    nt = (((1,), (1,)), ((), ()))
    for hd in range(HEADS_PER_KV):
        q = q_ref[:, hd * HEAD_DIM:(hd + 1) * HEAD_DIM]
        if hd == 0 and tk % (2 * V7X_MXU_DIM) == 0:
            s = jnp.concatenate([lax.dot_general(q, k[:tk // 2], nt, preferred_element_type=F32),
                                 lax.dot_general(q, k[tk // 2:], nt, preferred_element_type=F32)], axis=1)
        else:
            s = lax.dot_general(q, k, nt, preferred_element_type=F32)
        m_prev = m_scr[hd]
        m_next = jnp.maximum(m_prev, jnp.max(s, axis=1, keepdims=True))
        p = jnp.exp2(s - jnp.tile(m_next, (1, lane_tiles))).astype(BF16)
        alpha = jnp.exp2(m_prev - m_next)
        if hd == HEADS_PER_KV - 1 and tq % 32 == 0:
            pv = jnp.concatenate([jnp.dot(p[:tq // 2], v, preferred_element_type=F32),
                                  jnp.dot(p[tq // 2:], v, preferred_element_type=F32)], axis=0)
        else:
            pv = jnp.dot(p, v, preferred_element_type=F32)
        acc_scr[hd] = jnp.tile(alpha, (1, 2)) * acc_scr[hd] + pv
        m_scr[hd] = m_next


def _attention(q, k, v, batch, seq):
    n = q.shape[0]
    tq = _block(seq, 1024)
    tk = _block(seq, 2048)
    nq = seq // tq
    group_w = HEADS_PER_KV * HEAD_DIM
    return pl.pallas_call(
        functools.partial(_attn_kernel, tk),
        grid=(batch, N_KV_HEADS, nq),
        in_specs=[
            pl.BlockSpec((tq, group_w), lambda b, g, qi: (b * nq + qi, g)),
            pl.BlockSpec((seq, HEAD_DIM), lambda b, g, qi: (b, g)),
            pl.BlockSpec((seq, 2 * HEAD_DIM), lambda b, g, qi: (b, g)),
        ],
        out_specs=pl.BlockSpec((tq, group_w), lambda b, g, qi: (b * nq + qi, g)),
        out_shape=jax.ShapeDtypeStruct((n, ATTN_WIDTH), F32),
        scratch_shapes=[
            pltpu.VMEM((HEADS_PER_KV, tq, HEAD_DIM), F32),
            pltpu.VMEM((HEADS_PER_KV, tq, 2 * HEAD_DIM), F32),
        ],
        compiler_params=_params("parallel", "parallel", "parallel"),
        name="flash_attention",
    )(q, k, v)


def _lru_kernel(reverse, tc, n_chunks, *refs):
    if reverse:
        (xc_ref, wg_ref, bg_ref, lam_ref, hf_ref, yr_ref, out_ref, a_scr, u_scr, carry_scr) = refs
    else:
        (xm_ref, xp_ref, xn_ref, cw_ref, cb_ref, wg_ref, bg_ref, lam_ref,
         out_ref, xc_out_ref, ext_scr, a_scr, u_scr, carry_scr) = refs
    c = pl.program_id(1)

    @pl.when(c == 0)
    def _():
        carry_scr[...] = jnp.zeros(carry_scr.shape, F32)

    if reverse:
        xc = xc_ref[...]
    else:
        halo = V7X_SUBLANES
        ext_scr[0:halo, :] = jnp.where(c == 0, 0.0, xp_ref[...])
        ext_scr[halo:halo + tc, :] = xm_ref[...]
        ext_scr[halo + tc:2 * halo + tc, :] = jnp.where(c == n_chunks - 1, 0.0, xn_ref[...])
        xc = cb_ref[...]
        for tap in range(CONV_W):
            start = halo + tap - CONV_PAD_LEFT
            xc = xc + ext_scr[start:start + tc, :] * cw_ref[tap:tap + 1, :]
        xc_out_ref[...] = xc

    xcb = xc.astype(BF16)
    za, zx = [], []
    for g in range(LRU_WIDTH // V7X_MXU_DIM):
        zg = jnp.dot(xcb[:, g * V7X_MXU_DIM:(g + 1) * V7X_MXU_DIM], wg_ref[g],
                     preferred_element_type=F32)
        za.append(zg[:, :V7X_MXU_DIM])
        zx.append(zg[:, V7X_MXU_DIM:])
    r = _sigmoid(jnp.concatenate(za, axis=1) + bg_ref[0:1, :])
    i = _sigmoid(jnp.concatenate(zx, axis=1) + bg_ref[1:2, :])
    lam = lam_ref[...]
    softplus_neg = jnp.maximum(-lam, 0.0) + jnp.log(1.0 + jnp.exp(-jnp.abs(lam)))
    a = jnp.exp((-LRU_C * softplus_neg) * r)
    u = jnp.sqrt(1.0 - a * a) * (i * xc)

    nt = tc // V7X_SUBLANES
    a3 = a.reshape(nt, V7X_SUBLANES, LRU_WIDTH)
    u3 = u.reshape(nt, V7X_SUBLANES, LRU_WIDTH)
    row = lax.broadcasted_iota(jnp.int32, a3.shape, 1)
    for s in (1, 2, 4):
        if reverse:
            shift, valid = V7X_SUBLANES - s, row < V7X_SUBLANES - s
        else:
            shift, valid = s, row >= s
        a_sh = pltpu.roll(a3, shift, 1)
        u_sh = pltpu.roll(u3, shift, 1)
        u3 = u3 + jnp.where(valid, a3 * u_sh, 0.0)
        a3 = jnp.where(valid, a3 * a_sh, a3)
    a_scr[...] = a3
    u_scr[...] = u3

    def tile_step(t, carry):
        tt = (nt - 1 - t) if reverse else t
        h = u_scr[tt] + a_scr[tt] * carry
        u_scr[tt] = h
        edge = h[0:1, :] if reverse else h[V7X_SUBLANES - 1:V7X_SUBLANES, :]
        return jnp.broadcast_to(edge, h.shape)

    carry_scr[...] = lax.fori_loop(0, nt, tile_step, carry_scr[...])
    h_all = u_scr[...].reshape(tc, LRU_WIDTH)
    if reverse:
        out_ref[...] = (h_all + hf_ref[...]) * _gelu(yr_ref[...])
    else:
        out_ref[...] = h_all


def _lru_direction(reverse, x_in, conv_w, conv_b, w_gates, b_gates, lam, batch, seq, h_fwd=None, yr=None):
    n = x_in.shape[0]
    tc = _block(seq, 256)
    n_chunks = seq // tc
    halo = V7X_SUBLANES
    per_tile = tc // halo
    n_tiles = n // halo

    def chunk_of(c):
        return (n_chunks - 1 - c) if reverse else c

    main = lambda b, c: (b * n_chunks + chunk_of(c), 0)
    prev = lambda b, c: (jnp.maximum((b * n_chunks + c) * per_tile - 1, 0), 0)
    nxt = lambda b, c: (jnp.minimum((b * n_chunks + c + 1) * per_tile, n_tiles - 1), 0)
    fixed2 = lambda b, c: (0, 0)
    fixed3 = lambda b, c: (0, 0, 0)
    block = pl.BlockSpec((tc, LRU_WIDTH), main)
    gate_specs = [
        pl.BlockSpec((LRU_WIDTH // V7X_MXU_DIM, V7X_MXU_DIM, 2 * V7X_MXU_DIM), fixed3),
        pl.BlockSpec((2, LRU_WIDTH), fixed2),
        pl.BlockSpec((1, LRU_WIDTH), fixed2),
    ]
    scan_scratch = [
        pltpu.VMEM((per_tile, halo, LRU_WIDTH), F32),
        pltpu.VMEM((per_tile, halo, LRU_WIDTH), F32),
        pltpu.VMEM((halo, LRU_WIDTH), F32),
    ]
    out_sds = jax.ShapeDtypeStruct((n, LRU_WIDTH), F32)
    if reverse:
        in_specs = [block] + gate_specs + [block, block]
        args = [x_in, w_gates, b_gates, lam, h_fwd, yr]
        out_specs, out_shape, scratch = block, out_sds, scan_scratch
    else:
        in_specs = [block, pl.BlockSpec((halo, LRU_WIDTH), prev), pl.BlockSpec((halo, LRU_WIDTH), nxt),
                    pl.BlockSpec((CONV_W, LRU_WIDTH), fixed2), pl.BlockSpec((1, LRU_WIDTH), fixed2)] + gate_specs
        args = [x_in, x_in, x_in, conv_w, conv_b, w_gates, b_gates, lam]
        out_specs, out_shape = [block, block], [out_sds, out_sds]
        scratch = [pltpu.VMEM((tc + 2 * halo, LRU_WIDTH), F32)] + scan_scratch
    return pl.pallas_call(
        functools.partial(_lru_kernel, reverse, tc, n_chunks),
        grid=(batch, n_chunks),
        in_specs=in_specs,
        out_specs=out_specs,
        out_shape=out_shape,
        scratch_shapes=scratch,
        compiler_params=_params("parallel", "arbitrary"),
        name="lru_bwd" if reverse else "lru_fwd",
    )(*args)


def _out_kernel(x_ref, attn_ref, lru_ref, ga_ref, gl_ref, wa_ref, wl_ref, gf_ref, wq_ref,
                x1_ref, xnt_ref, qp_ref):
    an = _rms(attn_ref[...], ga_ref[...]).astype(BF16)
    ln = _rms(lru_ref[...], gl_ref[...]).astype(BF16)
    x1 = (x_ref[...] + jnp.dot(an, wa_ref[...], preferred_element_type=F32)
          + jnp.dot(ln, wl_ref[...], preferred_element_type=F32))
    x1_ref[...] = x1
    xn = _rms(x1, gf_ref[...])
    xnt_ref[...] = xn.T.astype(BF16)
    qp_ref[...] = jnp.dot(xn.astype(BF16), wq_ref[...], preferred_element_type=F32)


def _out_proj(x2, attn, lru, g_attn, g_lru, w_out_a, w_out_l, g_ffn, w_query):
    n = x2.shape[0]
    tm = _block(n, 512)
    qw = PEER_HEADS * PEER_KEY_DIM
    row = lambda i: (i, 0)
    fixed = lambda i: (0, 0)
    return pl.pallas_call(
        _out_kernel,
        grid=(n // tm,),
        in_specs=[
            pl.BlockSpec((tm, D_MODEL), row),
            pl.BlockSpec((tm, ATTN_WIDTH), row),
            pl.BlockSpec((tm, LRU_WIDTH), row),
            pl.BlockSpec((1, ATTN_WIDTH), fixed),
            pl.BlockSpec((1, LRU_WIDTH), fixed),
            pl.BlockSpec((ATTN_WIDTH, D_MODEL), fixed),
            pl.BlockSpec((LRU_WIDTH, D_MODEL), fixed),
            pl.BlockSpec((1, D_MODEL), fixed),
            pl.BlockSpec((D_MODEL, qw), fixed),
        ],
        out_specs=[
            pl.BlockSpec((tm, D_MODEL), row),
            pl.BlockSpec((D_MODEL, tm), lambda i: (0, i)),
            pl.BlockSpec((tm, qw), row),
        ],
        out_shape=[
            jax.ShapeDtypeStruct((n, D_MODEL), F32),
            jax.ShapeDtypeStruct((D_MODEL, n), BF16),
            jax.ShapeDtypeStruct((n, qw), F32),
        ],
        compiler_params=_params("parallel"),
        name="out_proj",
    )(x2, attn, lru, g_attn, g_lru, w_out_a, w_out_l, g_ffn, w_query)


def _oddeven_merge(lo, hi, r):
    step = r * 2
    if step < hi - lo:
        yield from _oddeven_merge(lo, hi, step)
        yield from _oddeven_merge(lo + r, hi, step)
        yield from [(i, i + r) for i in range(lo + r, hi - r, step)]
    else:
        yield (lo, lo + r)


def _oddeven_merge_sort(lo, hi):
    if hi - lo >= 1:
        mid = lo + (hi - lo) // 2
        yield from _oddeven_merge_sort(lo, mid)
        yield from _oddeven_merge_sort(mid + 1, hi)
        yield from _oddeven_merge(lo, hi, 1)


def _pop_lists(lists, count):
    lists = list(lists)
    vals = []
    for it in range(count):
        m = jnp.max(lists[0], axis=0, keepdims=True)
        vals.append(m)
        hit = lists[0] == m
        last = min(len(lists), count - it) - 1
        if it < count - 1:
            for c in range(last):
                lists[c] = jnp.where(hit, lists[c + 1], lists[c])
            lists[last] = jnp.where(hit, NEG_INF, lists[last])
    return vals


def _top_values(s, count):
    tiles = [s[v * V7X_SUBLANES:(v + 1) * V7X_SUBLANES, :] for v in range(s.shape[0] // V7X_SUBLANES)]
    for i, j in _oddeven_merge_sort(0, len(tiles) - 1):
        tiles[i], tiles[j] = jnp.maximum(tiles[i], tiles[j]), jnp.minimum(tiles[i], tiles[j])
    return _pop_lists(tiles, count)


def _peer_score_kernel(qp_ref, keys_ref, n1_ref, e1_ref, c2_ref, e2_ref):
    half_tile = V7X_SUBLANES
    sub = lax.broadcasted_iota(jnp.int32, (half_tile, qp_ref.shape[0]), 0)
    for hd in range(PEER_HEADS):
        qh = qp_ref[:, hd * PEER_KEY_DIM:(hd + 1) * PEER_KEY_DIM].astype(BF16)
        nt = (((1,), (1,)), ((), ()))
        s1 = lax.dot_general(keys_ref[hd, 0], qh, nt, preferred_element_type=F32)
        s2 = lax.dot_general(keys_ref[hd, 1], qh, nt, preferred_element_type=F32)
        top1 = _top_values(s1, PEER_TOPK)
        top2 = _top_values(s2, PEER_TOPK)
        a_lo = jnp.concatenate(top1[:half_tile], axis=0)
        a_hi = jnp.concatenate(top1[half_tile:], axis=0)
        lists = []
        for c in range(PEER_TOPK):
            keep = PEER_TOPK // (c + 1)
            row_sums = a_lo + top2[c]
            lists.append(row_sums if keep >= half_tile else jnp.where(sub < keep, row_sums, NEG_INF))
        heads_hi = a_hi + top2[0]
        m12 = top1[0] + top2[0]
        n_lo = jnp.zeros_like(a_lo)
        n_hi = jnp.zeros_like(a_lo)
        z_lo = jnp.zeros_like(a_lo)
        z_hi = jnp.zeros_like(a_lo)
        for it in range(PEER_TOPK):
            m = jnp.maximum(jnp.max(lists[0], axis=0, keepdims=True), jnp.max(heads_hi, axis=0, keepdims=True))
            e = jnp.exp(m - m12)
            hit_lo = lists[0] == m
            hit_hi = heads_hi == m
            n_lo = n_lo + jnp.where(hit_lo, 1.0, 0.0)
            n_hi = n_hi + jnp.where(hit_hi, 1.0, 0.0)
            z_lo = z_lo + jnp.where(hit_lo, e, 0.0)
            z_hi = z_hi + jnp.where(hit_hi, e, 0.0)
            if it < PEER_TOPK - 1:
                last = PEER_TOPK - it - 1
                for c in range(last):
                    lists[c] = jnp.where(hit_lo, lists[c + 1], lists[c])
                lists[last] = jnp.where(hit_lo, NEG_INF, lists[last])
                heads_hi = jnp.where(hit_hi, NEG_INF, heads_hi)
        z = jnp.sum(z_lo + z_hi, axis=0, keepdims=True)
        n1 = jnp.zeros_like(s1)
        rank2 = jnp.zeros_like(s2)
        for r in range(PEER_TOPK):
            n_r = (n_lo if r < half_tile else n_hi)[r % half_tile:r % half_tile + 1, :]
            n1 = jnp.where(s1 == top1[r], n_r, n1)
            rank2 = rank2 + jnp.where(s2 < top2[r], 1.0, 0.0)
        n1_ref[hd] = n1
        e1_ref[hd] = jnp.exp(s1 - top1[0]) / z
        c2_ref[hd] = rank2.astype(BF16)
        e2_ref[hd] = jnp.exp(s2 - top2[0]).astype(BF16)


def _peer_scores(qp, keys_pad):
    n = qp.shape[0]
    tb = _block(n, 256)
    qw = PEER_HEADS * PEER_KEY_DIM
    big = pl.BlockSpec((PEER_HEADS, N_KEYS, tb), lambda i: (0, 0, i))
    shape_f32 = jax.ShapeDtypeStruct((PEER_HEADS, N_KEYS, n), F32)
    shape_bf16 = jax.ShapeDtypeStruct((PEER_HEADS, N_KEYS, n), BF16)
    return pl.pallas_call(
        _peer_score_kernel,
        grid=(n // tb,),
        in_specs=[
            pl.BlockSpec((tb, qw), lambda i: (i, 0)),
            pl.BlockSpec((PEER_HEADS, 2, N_KEYS, PEER_KEY_DIM), lambda i: (0, 0, 0, 0)),
        ],
        out_specs=[big, big, big, big],
        out_shape=[shape_f32, shape_f32, shape_bf16, shape_bf16],
        compiler_params=_params("parallel"),
        name="peer_scores",
    )(qp, keys_pad)


PEER_KEYS_PER_STEP = V7X_SUBLANES
PEER_EXPERTS_PER_STEP = PEER_KEYS_PER_STEP * N_KEYS


def _peer_gate(n1_ref, e1_ref, c2_ref, e2_ref, g_scr):
    tb = g_scr.shape[1]
    bf16_rows = 2 * V7X_SUBLANES

    def key_row(ref, hd, il):
        row = jnp.broadcast_to(ref[hd, il:il + 1, :], (bf16_rows, tb)).astype(BF16)
        return jnp.tile(row, (N_KEYS // bf16_rows, 1))

    for il in range(PEER_KEYS_PER_STEP):
        gate = jnp.zeros((N_KEYS, tb), BF16)
        for hd in range(PEER_HEADS):
            n_row = key_row(n1_ref, hd, il)
            gate = gate + jnp.where(c2_ref[hd] < n_row, e2_ref[hd], 0.0) * key_row(e1_ref, hd, il)
        g_scr[il * N_KEYS:(il + 1) * N_KEYS, :] = gate


PEER_FIRST_MATMULS = 4
PEER_SECOND_SPLIT = 2


def _peer_mix_kernel(xnt_ref, ed_ref, eu_ref, n1_ref, e1_ref, c2_ref, e2_ref, x1_ref, y_ref, g_scr, w_scr):
    ec = pl.program_id(1)

    @pl.when(ec == 0)
    def _():
        y_ref[...] = x1_ref[...]

    _peer_gate(n1_ref, e1_ref, c2_ref, e2_ref, g_scr)
    tn = (((0,), (0,)), ((), ()))
    n_h = PEER_FIRST_MATMULS
    piece = ed_ref.shape[0] // n_h
    pieces = [slice(c * piece, (c + 1) * piece) for c in range(n_h)]
    hids = [jnp.dot(ed_ref[rows, :], xnt_ref[...], preferred_element_type=F32) for rows in pieces]
    parts = []
    for lo, hi in ((0, PEER_SECOND_SPLIT), (PEER_SECOND_SPLIT, n_h)):
        for q in range(lo, hi):
            w_scr[pieces[q], :] = _gelu(hids[q].astype(BF16)) * g_scr[pieces[q], :]
        rows = slice(lo * piece, hi * piece)
        parts.append(lax.dot_general(w_scr[rows, :], eu_ref[rows, :], tn, preferred_element_type=F32))
    y_ref[...] += parts[0] + parts[1]


def _peer_mix(xnt, e_down, e_up, n1, e1, c2, e2, x1):
    n = xnt.shape[1]
    tb = _block(n, 512)
    ec = PEER_EXPERTS_PER_STEP
    tok = lambda t, e: (t, 0)
    exp_ = lambda t, e: (e, 0)
    key1 = pl.BlockSpec((PEER_HEADS, PEER_KEYS_PER_STEP, tb), lambda t, e: (0, e, t))
    key2 = pl.BlockSpec((PEER_HEADS, N_KEYS, tb), lambda t, e: (0, 0, t))
    return pl.pallas_call(
        _peer_mix_kernel,
        grid=(n // tb, N_EXPERTS // ec),
        in_specs=[
            pl.BlockSpec((D_MODEL, tb), lambda t, e: (0, t)),
            pl.BlockSpec((ec, D_MODEL), exp_),
            pl.BlockSpec((ec, D_MODEL), exp_),
            key1, key1, key2, key2,
            pl.BlockSpec((tb, D_MODEL), tok),
        ],
        out_specs=pl.BlockSpec((tb, D_MODEL), tok),
        out_shape=jax.ShapeDtypeStruct((n, D_MODEL), F32),
        scratch_shapes=[pltpu.VMEM((ec, tb), BF16), pltpu.VMEM((ec, tb), BF16)],
        compiler_params=_params("parallel", "arbitrary"),
        name="peer_mix",
    )(xnt, e_down, e_up, n1, e1, c2, e2, x1)


def _rope_tables(seq):
    t = jnp.arange(seq, dtype=jnp.int32)
    row = (t // GRID_W).astype(F32)
    col = (t % GRID_W).astype(F32)
    half = HEAD_DIM // 2
    inv = ROPE_THETA ** (-jnp.arange(0, half, 2, dtype=F32) / half)
    ar = row[:, None] * inv
    ac = col[:, None] * inv
    cos_t = jnp.concatenate([jnp.cos(ar), jnp.cos(ar), jnp.cos(ac), jnp.cos(ac)], axis=-1)
    sin_t = jnp.concatenate([-jnp.sin(ar), jnp.sin(ar), -jnp.sin(ac), jnp.sin(ac)], axis=-1)
    return cos_t, sin_t


def _gate_weights(w_gate_a, w_gate_x):
    per_tile = V7X_MXU_DIM // LRU_BLOCK_W
    n_tiles = LRU_WIDTH // V7X_MXU_DIM

    def tiles(w):
        w = w.reshape(n_tiles, per_tile, LRU_BLOCK_W, LRU_BLOCK_W)
        eye = jnp.eye(per_tile, dtype=w.dtype)
        return jnp.einsum("tpkj,pq->tpkqj", w, eye).reshape(n_tiles, V7X_MXU_DIM, V7X_MXU_DIM)

    return jnp.concatenate([tiles(w_gate_a), tiles(w_gate_x)], axis=-1).astype(BF16)


def _padded_keys(sub_keys):
    z = jnp.zeros_like(sub_keys[:, 0])
    k0 = jnp.concatenate([sub_keys[:, 0], z], axis=-1)
    k1 = jnp.concatenate([z, sub_keys[:, 1]], axis=-1)
    return jnp.stack([k0, k1], axis=1).astype(BF16)


def _layer(x2, p, batch, seq, cos_t, sin_t):
    q, k, v = _qkv_proj(x2, p["g_mix"], p["w_qkv"], p["g_q"], p["g_k"], cos_t, sin_t, seq)
    attn = _attention(q, k, v, batch, seq)
    xr, yr = _xy_proj(x2, p["g_mix"], p["w_xy"])
    h_fwd, xc = _lru_direction(False, xr, p["conv_w"], p["conv_b"], p["w_gates"][0], p["b_gates"][0],
                               p["lam"][0], batch, seq)
    lru = _lru_direction(True, xc, p["conv_w"], p["conv_b"], p["w_gates"][1], p["b_gates"][1],
                         p["lam"][1], batch, seq, h_fwd=h_fwd, yr=yr)
    x1, xnt, qp = _out_proj(x2, attn, lru, p["g_attn_out"], p["g_lru_out"], p["w_out_a"], p["w_out_l"],
                           p["g_ffn"], p["w_query"])
    n1, e1, c2, e2 = _peer_scores(qp, p["keys_pad"])
    return _peer_mix(xnt, p["e_down"], p["e_up"], n1, e1, c2, e2, x1)


def _trunk(x, layers):
    batch, seq, _ = x.shape
    x2 = x.reshape(batch * seq, D_MODEL)
    cos_t, sin_t = _rope_tables(seq)
    for p in layers:
        x2 = _layer(x2, p, batch, seq, cos_t, sin_t)
    return x2.reshape(batch, seq, D_MODEL)


def _layer_params(l, g_mix, w_in, g_q, g_k, conv_w, conv_b, w_gate_a, b_gate_a, w_gate_x, b_gate_x, lru_lambda,
                  g_attn_out, g_lru_out, w_out, g_ffn, w_query, sub_keys, expert_down, expert_up):
    w_in_b = w_in[l].astype(BF16)
    w_out_b = w_out[l].astype(BF16)
    return {
        "g_mix": g_mix[l][None, :],
        "w_qkv": w_in_b[:, :QKV_WIDTH],
        "w_xy": w_in_b[:, QKV_WIDTH:],
        "g_q": g_q[l][None, :],
        "g_k": g_k[l][None, :],
        "conv_w": conv_w[l],
        "conv_b": conv_b[l][None, :],
        "w_gates": [_gate_weights(w_gate_a[l, d], w_gate_x[l, d]) for d in range(2)],
        "b_gates": [jnp.stack([b_gate_a[l, d], b_gate_x[l, d]], axis=0) for d in range(2)],
        "lam": [lru_lambda[l, d][None, :] for d in range(2)],
        "g_attn_out": g_attn_out[l][None, :],
        "g_lru_out": g_lru_out[l][None, :],
        "w_out_a": w_out_b[:ATTN_WIDTH],
        "w_out_l": w_out_b[ATTN_WIDTH:],
        "g_ffn": g_ffn[l][None, :],
        "w_query": w_query[l].astype(BF16),
        "keys_pad": _padded_keys(sub_keys[l]),
        "e_down": expert_down[l].astype(BF16),
        "e_up": expert_up[l].astype(BF16),
    }


def kernel(x_prompt, x_sample, g_mix, w_in, g_q, g_k, conv_w, conv_b, w_gate_a, b_gate_a, w_gate_x, b_gate_x, lru_lambda, g_attn_out, g_lru_out, w_out, g_ffn, w_query, sub_keys, expert_down, expert_up):
    weights = (g_mix, w_in, g_q, g_k, conv_w, conv_b, w_gate_a, b_gate_a, w_gate_x, b_gate_x, lru_lambda,
               g_attn_out, g_lru_out, w_out, g_ffn, w_query, sub_keys, expert_down, expert_up)
    layers = [_layer_params(l, *weights) for l in range(w_in.shape[0])]
    return (_trunk(x_prompt, layers), _trunk(x_sample, layers))
```

```python
import functools
import math

import jax
import jax.numpy as jnp
from jax import lax
from jax.experimental import pallas as pl
from jax.experimental.pallas import tpu as pltpu

F32 = jnp.float32
BF16 = jnp.bfloat16

D_MODEL = 2048
GRID_W = 64
N_HEADS = 8
N_KV_HEADS = 2
HEAD_DIM = 128
HEADS_PER_KV = N_HEADS // N_KV_HEADS
ATTN_WIDTH = N_HEADS * HEAD_DIM
KV_WIDTH = N_KV_HEADS * HEAD_DIM
QKV_WIDTH = ATTN_WIDTH + 2 * KV_WIDTH
ROPE_THETA = 10000.0
LRU_WIDTH = D_MODEL - ATTN_WIDTH
LRU_BLOCK_W = 64
CONV_W = 4
CONV_PAD_LEFT = 2
LRU_C = 8.0
N_KEYS = 128
N_EXPERTS = N_KEYS * N_KEYS
PEER_HEADS = 8
PEER_KEY_DIM = 128
PEER_HALF = PEER_KEY_DIM // 2
PEER_TOPK = 16
EPS = 1e-6

V7X_SUBLANES = 8
V7X_LANES = 128
V7X_MXU_DIM = 256
V7X_VMEM_LIMIT_BYTES = 56 * 1024 * 1024

NEG_INF = float("-inf")


def _params(*semantics, flags=None):
    return pltpu.CompilerParams(dimension_semantics=semantics,
                                vmem_limit_bytes=V7X_VMEM_LIMIT_BYTES,
                                flags=flags)


def _block(n, target):
    b = min(n, target)
    while n % b:
        b //= 2
    return b


def _gelu(x):
    return 0.5 * x * (1.0 + lax.erf(x * (1.0 / math.sqrt(2.0))))


def _sigmoid(x):
    return 0.5 + 0.5 * jnp.tanh(0.5 * x)


def _rms(x, g):
    return x * lax.rsqrt(jnp.mean(x * x, axis=-1, keepdims=True) + EPS) * g


def _qkv_kernel(x_ref, gmix_ref, w_ref, gq_ref, gk_ref, cos_ref, sin_ref,
                q_ref, k_ref, v_ref):
    tm = x_ref.shape[0]
    n_sub = 2 if tm % (4 * V7X_SUBLANES) == 0 else 1
    halves = [slice(c * (tm // n_sub), (c + 1) * (tm // n_sub)) for c in range(n_sub)]
    zs = [jnp.dot(_rms(x_ref[rows, :], gmix_ref[...]).astype(BF16), w_ref[...], preferred_element_type=F32)
          for rows in halves]
    lane = lax.broadcasted_iota(jnp.int32, (tm // n_sub, HEAD_DIM), 1)
    first_half = (lane % (HEAD_DIM // 2)) < (HEAD_DIM // 4)
    scale = HEAD_DIM ** -0.5 * math.log2(math.e)
    gq = gq_ref[...]
    gk = gk_ref[...]
    for rows, z in zip(halves, zs):
        cos = cos_ref[rows, :]
        sin = sin_ref[rows, :]

        def norm_rope(zh, g):
            y = _rms(zh, g)
            partner = jnp.where(first_half,
                                pltpu.roll(y, HEAD_DIM - HEAD_DIM // 4, 1),
                                pltpu.roll(y, HEAD_DIM // 4, 1))
            return y * cos + partner * sin

        for hd in range(N_HEADS):
            sl = slice(hd * HEAD_DIM, (hd + 1) * HEAD_DIM)
            q_ref[rows, sl] = (norm_rope(z[:, sl], gq) * scale).astype(BF16)
        for hd in range(N_KV_HEADS):
            sl = slice(hd * HEAD_DIM, (hd + 1) * HEAD_DIM)
            k_ref[rows, sl] = norm_rope(z[:, ATTN_WIDTH + hd * HEAD_DIM:ATTN_WIDTH + (hd + 1) * HEAD_DIM], gk).astype(BF16)
        ones = jnp.ones((z.shape[0], HEAD_DIM), BF16)
        for hd in range(N_KV_HEADS):
            v0 = ATTN_WIDTH + KV_WIDTH + hd * HEAD_DIM
            v_ref[rows, 2 * hd * HEAD_DIM:(2 * hd + 1) * HEAD_DIM] = z[:, v0:v0 + HEAD_DIM].astype(BF16)
            v_ref[rows, (2 * hd + 1) * HEAD_DIM:(2 * hd + 2) * HEAD_DIM] = ones


def _qkv_proj(x2, g_mix, w_qkv, g_q, g_k, cos_t, sin_t, seq):
    n = x2.shape[0]
    tm = _block(seq, 512)
    nseq_blocks = seq // tm
    row = lambda i: (i, 0)
    fixed = lambda i: (0, 0)
    pos = lambda i: (i % nseq_blocks, 0)
    return pl.pallas_call(
        _qkv_kernel,
        grid=(n // tm,),
        in_specs=[
            pl.BlockSpec((tm, D_MODEL), row),
            pl.BlockSpec((1, D_MODEL), fixed),
            pl.BlockSpec((D_MODEL, QKV_WIDTH), fixed),
            pl.BlockSpec((1, HEAD_DIM), fixed),
            pl.BlockSpec((1, HEAD_DIM), fixed),
            pl.BlockSpec((tm, HEAD_DIM), pos),
            pl.BlockSpec((tm, HEAD_DIM), pos),
        ],
        out_specs=[
            pl.BlockSpec((tm, ATTN_WIDTH), row),
            pl.BlockSpec((tm, KV_WIDTH), row),
            pl.BlockSpec((tm, 2 * KV_WIDTH), row),
        ],
        out_shape=[
            jax.ShapeDtypeStruct((n, ATTN_WIDTH), BF16),
            jax.ShapeDtypeStruct((n, KV_WIDTH), BF16),
            jax.ShapeDtypeStruct((n, 2 * KV_WIDTH), BF16),
        ],
        compiler_params=_params("parallel"),
        name="qkv_proj",
    )(x2, g_mix, w_qkv, g_q, g_k, cos_t, sin_t)


def _xy_kernel(x_ref, gmix_ref, w_ref, xr_ref, yr_ref):
    h = _rms(x_ref[...], gmix_ref[...]).astype(BF16)
    z = jnp.dot(h, w_ref[...], preferred_element_type=F32)
    xr_ref[...] = z[:, :LRU_WIDTH]
    yr_ref[...] = z[:, LRU_WIDTH:]


def _xy_proj(x2, g_mix, w_xy):
    n = x2.shape[0]
    tm = _block(n, 512)
    row = lambda i: (i, 0)
    fixed = lambda i: (0, 0)
    return pl.pallas_call(
        _xy_kernel,
        grid=(n // tm,),
        in_specs=[
            pl.BlockSpec((tm, D_MODEL), row),
            pl.BlockSpec((1, D_MODEL), fixed),
            pl.BlockSpec((D_MODEL, 2 * LRU_WIDTH), fixed),
        ],
        out_specs=[pl.BlockSpec((tm, LRU_WIDTH), row), pl.BlockSpec((tm, LRU_WIDTH), row)],
        out_shape=[jax.ShapeDtypeStruct((n, LRU_WIDTH), F32)] * 2,
        compiler_params=_params("parallel"),
        name="xy_proj",
    )(x2, g_mix, w_xy)


def _attn_kernel(tk, q_ref, k_ref, v_ref, o_ref, m_scr, acc_scr):
    m_scr[...] = jnp.full(m_scr.shape, NEG_INF, F32)
    acc_scr[...] = jnp.zeros(acc_scr.shape, F32)
    tq = q_ref.shape[0]
    lane_tiles = tk // HEAD_DIM

    def kv_block(ki, carry):
        start = pl.multiple_of(ki * tk, tk)
        _attn_block(q_ref, k_ref[pl.ds(start, tk), :], v_ref[pl.ds(start, tk), :], m_scr, acc_scr, tq, tk, lane_tiles)
        return carry

    lax.fori_loop(0, k_ref.shape[0] // tk, kv_block, 0)
    for hd in range(HEADS_PER_KV):
        acc = acc_scr[hd]
        o_ref[:, hd * HEAD_DIM:(hd + 1) * HEAD_DIM] = acc[:, :HEAD_DIM] / acc[:, HEAD_DIM:]


def _attn_block(q_ref, k, v, m_scr, acc_scr, tq, tk, lane_tiles):
    nt = (((1,), (1,)), ((), ()))
    for hd in range(HEADS_PER_KV):
        q = q_ref[:, hd * HEAD_DIM:(hd + 1) * HEAD_DIM]
        if hd == 0 and tk % (2 * V7X_MXU_DIM) == 0:
            s = jnp.concatenate([lax.dot_general(q, k[:tk // 2], nt, preferred_element_type=F32),
                                 lax.dot_general(q, k[tk // 2:], nt, preferred_element_type=F32)], axis=1)
        else:
            s = lax.dot_general(q, k, nt, preferred_element_type=F32)
        m_prev = m_scr[hd]
        m_next = jnp.maximum(m_prev, jnp.max(s, axis=1, keepdims=True))
        p = jnp.exp2(s - jnp.tile(m_next, (1, lane_tiles))).astype(BF16)
        alpha = jnp.exp2(m_prev - m_next)
        if hd == HEADS_PER_KV - 1 and tq % 32 == 0:
            pv = jnp.concatenate([jnp.dot(p[:tq // 2], v, preferred_element_type=F32),
                                  jnp.dot(p[tq // 2:], v, preferred_element_type=F32)], axis=0)
        else:
            pv = jnp.dot(p, v, preferred_element_type=F32)
        acc_scr[hd] = jnp.tile(alpha, (1, 2)) * acc_scr[hd] + pv
        m_scr[hd] = m_next


def _attention(q, k, v, batch, seq):
    n = q.shape[0]
    tq = _block(seq, 1024)
    tk = _block(seq, 2048)
    nq = seq // tq
    group_w = HEADS_PER_KV * HEAD_DIM
    return pl.pallas_call(
        functools.partial(_attn_kernel, tk),
        grid=(batch, N_KV_HEADS, nq),
        in_specs=[
            pl.BlockSpec((tq, group_w), lambda b, g, qi: (b * nq + qi, g)),
            pl.BlockSpec((seq, HEAD_DIM), lambda b, g, qi: (b, g)),
            pl.BlockSpec((seq, 2 * HEAD_DIM), lambda b, g, qi: (b, g)),
        ],
        out_specs=pl.BlockSpec((tq, group_w), lambda b, g, qi: (b * nq + qi, g)),
        out_shape=jax.ShapeDtypeStruct((n, ATTN_WIDTH), F32),
        scratch_shapes=[
            pltpu.VMEM((HEADS_PER_KV, tq, HEAD_DIM), F32),
            pltpu.VMEM((HEADS_PER_KV, tq, 2 * HEAD_DIM), F32),
        ],
        compiler_params=_params("parallel", "parallel", "parallel"),
        name="flash_attention",
    )(q, k, v)


def _lru_kernel(reverse, tc, n_chunks, *refs):
    if reverse:
        (xc_ref, wg_ref, bg_ref, lam_ref, hf_ref, yr_ref, out_ref, a_scr, u_scr, carry_scr) = refs
    else:
        (xm_ref, xp_ref, xn_ref, cw_ref, cb_ref, wg_ref, bg_ref, lam_ref,
         out_ref, xc_out_ref, ext_scr, a_scr, u_scr, carry_scr) = refs
    c = pl.program_id(1)

    @pl.when(c == 0)
    def _():
        carry_scr[...] = jnp.zeros(carry_scr.shape, F32)

    if reverse:
        xc = xc_ref[...]
    else:
        halo = V7X_SUBLANES
        ext_scr[0:halo, :] = jnp.where(c == 0, 0.0, xp_ref[...])
        ext_scr[halo:halo + tc, :] = xm_ref[...]
        ext_scr[halo + tc:2 * halo + tc, :] = jnp.where(c == n_chunks - 1, 0.0, xn_ref[...])
        xc = cb_ref[...]
        for tap in range(CONV_W):
            start = halo + tap - CONV_PAD_LEFT
            xc = xc + ext_scr[start:start + tc, :] * cw_ref[tap:tap + 1, :]
        xc_out_ref[...] = xc

    xcb = xc.astype(BF16)
    za, zx = [], []
    for g in range(LRU_WIDTH // V7X_MXU_DIM):
        zg = jnp.dot(xcb[:, g * V7X_MXU_DIM:(g + 1) * V7X_MXU_DIM], wg_ref[g],
                     preferred_element_type=F32)
        za.append(zg[:, :V7X_MXU_DIM])
        zx.append(zg[:, V7X_MXU_DIM:])
    r = _sigmoid(jnp.concatenate(za, axis=1) + bg_ref[0:1, :])
    i = _sigmoid(jnp.concatenate(zx, axis=1) + bg_ref[1:2, :])
    lam = lam_ref[...]
    softplus_neg = jnp.maximum(-lam, 0.0) + jnp.log(1.0 + jnp.exp(-jnp.abs(lam)))
    a = jnp.exp((-LRU_C * softplus_neg) * r)
    u = jnp.sqrt(1.0 - a * a) * (i * xc)

    nt = tc // V7X_SUBLANES
    a3 = a.reshape(nt, V7X_SUBLANES, LRU_WIDTH)
    u3 = u.reshape(nt, V7X_SUBLANES, LRU_WIDTH)
    row = lax.broadcasted_iota(jnp.int32, a3.shape, 1)
    for s in (1, 2, 4):
        if reverse:
            shift, valid = V7X_SUBLANES - s, row < V7X_SUBLANES - s
        else:
            shift, valid = s, row >= s
        a_sh = pltpu.roll(a3, shift, 1)
        u_sh = pltpu.roll(u3, shift, 1)
        u3 = u3 + jnp.where(valid, a3 * u_sh, 0.0)
        a3 = jnp.where(valid, a3 * a_sh, a3)
    a_scr[...] = a3
    u_scr[...] = u3

    def tile_step(t, carry):
        tt = (nt - 1 - t) if reverse else t
        h = u_scr[tt] + a_scr[tt] * carry
        u_scr[tt] = h
        edge = h[0:1, :] if reverse else h[V7X_SUBLANES - 1:V7X_SUBLANES, :]
        return jnp.broadcast_to(edge, h.shape)

    carry_scr[...] = lax.fori_loop(0, nt, tile_step, carry_scr[...])
    h_all = u_scr[...].reshape(tc, LRU_WIDTH)
    if reverse:
        out_ref[...] = (h_all + hf_ref[...]) * _gelu(yr_ref[...])
    else:
        out_ref[...] = h_all


def _lru_direction(reverse, x_in, conv_w, conv_b, w_gates, b_gates, lam, batch, seq, h_fwd=None, yr=None):
    n = x_in.shape[0]
    tc = _block(seq, 512)
    n_chunks = seq // tc
    halo = V7X_SUBLANES
    per_tile = tc // halo
    n_tiles = n // halo

    def chunk_of(c):
        return (n_chunks - 1 - c) if reverse else c

    main = lambda b, c: (b * n_chunks + chunk_of(c), 0)
    prev = lambda b, c: (jnp.maximum((b * n_chunks + c) * per_tile - 1, 0), 0)
    nxt = lambda b, c: (jnp.minimum((b * n_chunks + c + 1) * per_tile, n_tiles - 1), 0)
    fixed2 = lambda b, c: (0, 0)
    fixed3 = lambda b, c: (0, 0, 0)
    block = pl.BlockSpec((tc, LRU_WIDTH), main)
    gate_specs = [
        pl.BlockSpec((LRU_WIDTH // V7X_MXU_DIM, V7X_MXU_DIM, 2 * V7X_MXU_DIM), fixed3),
        pl.BlockSpec((2, LRU_WIDTH), fixed2),
        pl.BlockSpec((1, LRU_WIDTH), fixed2),
    ]
    scan_scratch = [
        pltpu.VMEM((per_tile, halo, LRU_WIDTH), F32),
        pltpu.VMEM((per_tile, halo, LRU_WIDTH), F32),
        pltpu.VMEM((halo, LRU_WIDTH), F32),
    ]
    out_sds = jax.ShapeDtypeStruct((n, LRU_WIDTH), F32)
    if reverse:
        in_specs = [block] + gate_specs + [block, block]
        args = [x_in, w_gates, b_gates, lam, h_fwd, yr]
        out_specs, out_shape, scratch = block, out_sds, scan_scratch
    else:
        in_specs = [block, pl.BlockSpec((halo, LRU_WIDTH), prev), pl.BlockSpec((halo, LRU_WIDTH), nxt),
                    pl.BlockSpec((CONV_W, LRU_WIDTH), fixed2), pl.BlockSpec((1, LRU_WIDTH), fixed2)] + gate_specs
        args = [x_in, x_in, x_in, conv_w, conv_b, w_gates, b_gates, lam]
        out_specs, out_shape = [block, block], [out_sds, out_sds]
        scratch = [pltpu.VMEM((tc + 2 * halo, LRU_WIDTH), F32)] + scan_scratch
    return pl.pallas_call(
        functools.partial(_lru_kernel, reverse, tc, n_chunks),
        grid=(batch, n_chunks),
        in_specs=in_specs,
        out_specs=out_specs,
        out_shape=out_shape,
        scratch_shapes=scratch,
        compiler_params=_params("parallel", "arbitrary"),
        name="lru_bwd" if reverse else "lru_fwd",
    )(*args)


def _out_kernel(x_ref, attn_ref, lru_ref, ga_ref, gl_ref, wa_ref, wl_ref, gf_ref, wq_ref,
                x1_ref, xnt_ref, qp_ref):
    an = _rms(attn_ref[...], ga_ref[...]).astype(BF16)
    ln = _rms(lru_ref[...], gl_ref[...]).astype(BF16)
    x1 = (x_ref[...] + jnp.dot(an, wa_ref[...], preferred_element_type=F32)
          + jnp.dot(ln, wl_ref[...], preferred_element_type=F32))
    x1_ref[...] = x1
    xn = _rms(x1, gf_ref[...])
    xnt_ref[...] = xn.T.astype(BF16)
    qp_ref[...] = jnp.dot(xn.astype(BF16), wq_ref[...], preferred_element_type=F32)


def _out_proj(x2, attn, lru, g_attn, g_lru, w_out_a, w_out_l, g_ffn, w_query):
    n = x2.shape[0]
    tm = _block(n, 512)
    qw = PEER_HEADS * PEER_KEY_DIM
    row = lambda i: (i, 0)
    fixed = lambda i: (0, 0)
    return pl.pallas_call(
        _out_kernel,
        grid=(n // tm,),
        in_specs=[
            pl.BlockSpec((tm, D_MODEL), row),
            pl.BlockSpec((tm, ATTN_WIDTH), row),
            pl.BlockSpec((tm, LRU_WIDTH), row),
            pl.BlockSpec((1, ATTN_WIDTH), fixed),
            pl.BlockSpec((1, LRU_WIDTH), fixed),
            pl.BlockSpec((ATTN_WIDTH, D_MODEL), fixed),
            pl.BlockSpec((LRU_WIDTH, D_MODEL), fixed),
            pl.BlockSpec((1, D_MODEL), fixed),
            pl.BlockSpec((D_MODEL, qw), fixed),
        ],
        out_specs=[
            pl.BlockSpec((tm, D_MODEL), row),
            pl.BlockSpec((D_MODEL, tm), lambda i: (0, i)),
            pl.BlockSpec((tm, qw), row),
        ],
        out_shape=[
            jax.ShapeDtypeStruct((n, D_MODEL), F32),
            jax.ShapeDtypeStruct((D_MODEL, n), BF16),
            jax.ShapeDtypeStruct((n, qw), F32),
        ],
        compiler_params=_params("parallel"),
        name="out_proj",
    )(x2, attn, lru, g_attn, g_lru, w_out_a, w_out_l, g_ffn, w_query)


def _oddeven_merge(lo, hi, r):
    step = r * 2
    if step < hi - lo:
        yield from _oddeven_merge(lo, hi, step)
        yield from _oddeven_merge(lo + r, hi, step)
        yield from [(i, i + r) for i in range(lo + r, hi - r, step)]
    else:
        yield (lo, lo + r)


def _oddeven_merge_sort(lo, hi):
    if hi - lo >= 1:
        mid = lo + (hi - lo) // 2
        yield from _oddeven_merge_sort(lo, mid)
        yield from _oddeven_merge_sort(mid + 1, hi)
        yield from _oddeven_merge(lo, hi, 1)


def _pop_lists(lists, count):
    lists = list(lists)
    vals = []
    for it in range(count):
        m = jnp.max(lists[0], axis=0, keepdims=True)
        vals.append(m)
        hit = lists[0] == m
        last = min(len(lists), count - it) - 1
        if it < count - 1:
            for c in range(last):
                lists[c] = jnp.where(hit, lists[c + 1], lists[c])
            lists[last] = jnp.where(hit, NEG_INF, lists[last])
    return vals


def _top_values(s, count):
    tiles = [s[v * V7X_SUBLANES:(v + 1) * V7X_SUBLANES, :] for v in range(s.shape[0] // V7X_SUBLANES)]
    for i, j in _oddeven_merge_sort(0, len(tiles) - 1):
        tiles[i], tiles[j] = jnp.maximum(tiles[i], tiles[j]), jnp.minimum(tiles[i], tiles[j])
    return _pop_lists(tiles, count)


def _peer_score_kernel(qp_ref, keys_ref, n1_ref, e1_ref, c2_ref, e2_ref):
    half_tile = V7X_SUBLANES
    sub = lax.broadcasted_iota(jnp.int32, (half_tile, qp_ref.shape[0]), 0)
    for hd in range(PEER_HEADS):
        qh = qp_ref[:, hd * PEER_KEY_DIM:(hd + 1) * PEER_KEY_DIM].astype(BF16)
        nt = (((1,), (1,)), ((), ()))
        s1 = lax.dot_general(keys_ref[hd, 0], qh, nt, preferred_element_type=F32)
        s2 = lax.dot_general(keys_ref[hd, 1], qh, nt, preferred_element_type=F32)
        top1 = _top_values(s1, PEER_TOPK)
        top2 = _top_values(s2, PEER_TOPK)
        a_lo = jnp.concatenate(top1[:half_tile], axis=0)
        a_hi = jnp.concatenate(top1[half_tile:], axis=0)
        lists = []
        for c in range(PEER_TOPK):
            keep = PEER_TOPK // (c + 1)
            row_sums = a_lo + top2[c]
            lists.append(row_sums if keep >= half_tile else jnp.where(sub < keep, row_sums, NEG_INF))
        heads_hi = a_hi + top2[0]
        m12 = top1[0] + top2[0]
        n_lo = jnp.zeros_like(a_lo)
        n_hi = jnp.zeros_like(a_lo)
        z_lo = jnp.zeros_like(a_lo)
        z_hi = jnp.zeros_like(a_lo)
        for it in range(PEER_TOPK):
            m = jnp.maximum(jnp.max(lists[0], axis=0, keepdims=True), jnp.max(heads_hi, axis=0, keepdims=True))
            e = jnp.exp(m - m12)
            hit_lo = lists[0] == m
            hit_hi = heads_hi == m
            n_lo = n_lo + jnp.where(hit_lo, 1.0, 0.0)
            n_hi = n_hi + jnp.where(hit_hi, 1.0, 0.0)
            z_lo = z_lo + jnp.where(hit_lo, e, 0.0)
            z_hi = z_hi + jnp.where(hit_hi, e, 0.0)
            if it < PEER_TOPK - 1:
                last = PEER_TOPK - it - 1
                for c in range(last):
                    lists[c] = jnp.where(hit_lo, lists[c + 1], lists[c])
                lists[last] = jnp.where(hit_lo, NEG_INF, lists[last])
                heads_hi = jnp.where(hit_hi, NEG_INF, heads_hi)
        z = jnp.sum(z_lo + z_hi, axis=0, keepdims=True)
        n1 = jnp.zeros_like(s1)
        rank2 = jnp.zeros_like(s2)
        for r in range(PEER_TOPK):
            n_r = (n_lo if r < half_tile else n_hi)[r % half_tile:r % half_tile + 1, :]
            n1 = jnp.where(s1 == top1[r], n_r, n1)
            rank2 = rank2 + jnp.where(s2 < top2[r], 1.0, 0.0)
        n1_ref[hd] = n1
        e1_ref[hd] = jnp.exp(s1 - top1[0]) / z
        c2_ref[hd] = rank2.astype(BF16)
        e2_ref[hd] = jnp.exp(s2 - top2[0]).astype(BF16)


def _peer_scores(qp, keys_pad):
    n = qp.shape[0]
    tb = _block(n, 256)
    qw = PEER_HEADS * PEER_KEY_DIM
    big = pl.BlockSpec((PEER_HEADS, N_KEYS, tb), lambda i: (0, 0, i))
    shape_f32 = jax.ShapeDtypeStruct((PEER_HEADS, N_KEYS, n), F32)
    shape_bf16 = jax.ShapeDtypeStruct((PEER_HEADS, N_KEYS, n), BF16)
    return pl.pallas_call(
        _peer_score_kernel,
        grid=(n // tb,),
        in_specs=[
            pl.BlockSpec((tb, qw), lambda i: (i, 0)),
            pl.BlockSpec((PEER_HEADS, 2, N_KEYS, PEER_KEY_DIM), lambda i: (0, 0, 0, 0)),
        ],
        out_specs=[big, big, big, big],
        out_shape=[shape_f32, shape_f32, shape_bf16, shape_bf16],
        compiler_params=_params("parallel"),
        name="peer_scores",
    )(qp, keys_pad)


PEER_KEYS_PER_STEP = V7X_SUBLANES
PEER_EXPERTS_PER_STEP = PEER_KEYS_PER_STEP * N_KEYS


def _peer_gate(n1_ref, e1_ref, c2_ref, e2_ref, g_scr):
    tb = g_scr.shape[1]
    bf16_rows = 2 * V7X_SUBLANES

    def key_row(ref, hd, il):
        row = jnp.broadcast_to(ref[hd, il:il + 1, :], (bf16_rows, tb)).astype(BF16)
        return jnp.tile(row, (N_KEYS // bf16_rows, 1))

    for il in range(PEER_KEYS_PER_STEP):
        gate = jnp.zeros((N_KEYS, tb), BF16)
        for hd in range(PEER_HEADS):
            n_row = key_row(n1_ref, hd, il)
            gate = gate + jnp.where(c2_ref[hd] < n_row, e2_ref[hd], 0.0) * key_row(e1_ref, hd, il)
        g_scr[il * N_KEYS:(il + 1) * N_KEYS, :] = gate


PEER_FIRST_MATMULS = 4
PEER_SECOND_SPLIT = 2


def _peer_mix_kernel(xnt_ref, ed_ref, eu_ref, n1_ref, e1_ref, c2_ref, e2_ref, x1_ref, y_ref, g_scr, w_scr):
    ec = pl.program_id(1)

    @pl.when(ec == 0)
    def _():
        y_ref[...] = x1_ref[...]

    _peer_gate(n1_ref, e1_ref, c2_ref, e2_ref, g_scr)
    tn = (((0,), (0,)), ((), ()))
    n_h = PEER_FIRST_MATMULS
    piece = ed_ref.shape[0] // n_h
    pieces = [slice(c * piece, (c + 1) * piece) for c in range(n_h)]
    hids = [jnp.dot(ed_ref[rows, :], xnt_ref[...], preferred_element_type=F32) for rows in pieces]
    parts = []
    for lo, hi in ((0, PEER_SECOND_SPLIT), (PEER_SECOND_SPLIT, n_h)):
        for q in range(lo, hi):
            w_scr[pieces[q], :] = _gelu(hids[q].astype(BF16)) * g_scr[pieces[q], :]
        rows = slice(lo * piece, hi * piece)
        parts.append(lax.dot_general(w_scr[rows, :], eu_ref[rows, :], tn, preferred_element_type=F32))
    y_ref[...] += parts[0] + parts[1]


def _peer_mix(xnt, e_down, e_up, n1, e1, c2, e2, x1):
    n = xnt.shape[1]
    tb = _block(n, 512)
    ec = PEER_EXPERTS_PER_STEP
    tok = lambda t, e: (t, 0)
    exp_ = lambda t, e: (e, 0)
    key1 = pl.BlockSpec((PEER_HEADS, PEER_KEYS_PER_STEP, tb), lambda t, e: (0, e, t))
    key2 = pl.BlockSpec((PEER_HEADS, N_KEYS, tb), lambda t, e: (0, 0, t))
    return pl.pallas_call(
        _peer_mix_kernel,
        grid=(n // tb, N_EXPERTS // ec),
        in_specs=[
            pl.BlockSpec((D_MODEL, tb), lambda t, e: (0, t)),
            pl.BlockSpec((ec, D_MODEL), exp_),
            pl.BlockSpec((ec, D_MODEL), exp_),
            key1, key1, key2, key2,
            pl.BlockSpec((tb, D_MODEL), tok),
        ],
        out_specs=pl.BlockSpec((tb, D_MODEL), tok),
        out_shape=jax.ShapeDtypeStruct((n, D_MODEL), F32),
        scratch_shapes=[pltpu.VMEM((ec, tb), BF16), pltpu.VMEM((ec, tb), BF16)],
        compiler_params=_params("parallel", "arbitrary"),
        name="peer_mix",
    )(xnt, e_down, e_up, n1, e1, c2, e2, x1)


def _rope_tables(seq):
    t = jnp.arange(seq, dtype=jnp.int32)
    row = (t // GRID_W).astype(F32)
    col = (t % GRID_W).astype(F32)
    half = HEAD_DIM // 2
    inv = ROPE_THETA ** (-jnp.arange(0, half, 2, dtype=F32) / half)
    ar = row[:, None] * inv
    ac = col[:, None] * inv
    cos_t = jnp.concatenate([jnp.cos(ar), jnp.cos(ar), jnp.cos(ac), jnp.cos(ac)], axis=-1)
    sin_t = jnp.concatenate([-jnp.sin(ar), jnp.sin(ar), -jnp.sin(ac), jnp.sin(ac)], axis=-1)
    return cos_t, sin_t


def _gate_weights(w_gate_a, w_gate_x):
    per_tile = V7X_MXU_DIM // LRU_BLOCK_W
    n_tiles = LRU_WIDTH // V7X_MXU_DIM

    def tiles(w):
        w = w.reshape(n_tiles, per_tile, LRU_BLOCK_W, LRU_BLOCK_W)
        eye = jnp.eye(per_tile, dtype=w.dtype)
        return jnp.einsum("tpkj,pq->tpkqj", w, eye).reshape(n_tiles, V7X_MXU_DIM, V7X_MXU_DIM)

    return jnp.concatenate([tiles(w_gate_a), tiles(w_gate_x)], axis=-1).astype(BF16)


def _padded_keys(sub_keys):
    z = jnp.zeros_like(sub_keys[:, 0])
    k0 = jnp.concatenate([sub_keys[:, 0], z], axis=-1)
    k1 = jnp.concatenate([z, sub_keys[:, 1]], axis=-1)
    return jnp.stack([k0, k1], axis=1).astype(BF16)


def _layer(x2, p, batch, seq, cos_t, sin_t):
    q, k, v = _qkv_proj(x2, p["g_mix"], p["w_qkv"], p["g_q"], p["g_k"], cos_t, sin_t, seq)
    attn = _attention(q, k, v, batch, seq)
    xr, yr = _xy_proj(x2, p["g_mix"], p["w_xy"])
    h_fwd, xc = _lru_direction(False, xr, p["conv_w"], p["conv_b"], p["w_gates"][0], p["b_gates"][0],
                               p["lam"][0], batch, seq)
    lru = _lru_direction(True, xc, p["conv_w"], p["conv_b"], p["w_gates"][1], p["b_gates"][1],
                         p["lam"][1], batch, seq, h_fwd=h_fwd, yr=yr)
    x1, xnt, qp = _out_proj(x2, attn, lru, p["g_attn_out"], p["g_lru_out"], p["w_out_a"], p["w_out_l"],
                           p["g_ffn"], p["w_query"])
    n1, e1, c2, e2 = _peer_scores(qp, p["keys_pad"])
    return _peer_mix(xnt, p["e_down"], p["e_up"], n1, e1, c2, e2, x1)


def _trunk(x, layers):
    batch, seq, _ = x.shape
    x2 = x.reshape(batch * seq, D_MODEL)
    cos_t, sin_t = _rope_tables(seq)
    for p in layers:
        x2 = _layer(x2, p, batch, seq, cos_t, sin_t)
    return x2.reshape(batch, seq, D_MODEL)


def _layer_params(l, g_mix, w_in, g_q, g_k, conv_w, conv_b, w_gate_a, b_gate_a, w_gate_x, b_gate_x, lru_lambda,
                  g_attn_out, g_lru_out, w_out, g_ffn, w_query, sub_keys, expert_down, expert_up):
    w_in_b = w_in[l].astype(BF16)
    w_out_b = w_out[l].astype(BF16)
    return {
        "g_mix": g_mix[l][None, :],
        "w_qkv": w_in_b[:, :QKV_WIDTH],
        "w_xy": w_in_b[:, QKV_WIDTH:],
        "g_q": g_q[l][None, :],
        "g_k": g_k[l][None, :],
        "conv_w": conv_w[l],
        "conv_b": conv_b[l][None, :],
        "w_gates": [_gate_weights(w_gate_a[l, d], w_gate_x[l, d]) for d in range(2)],
        "b_gates": [jnp.stack([b_gate_a[l, d], b_gate_x[l, d]], axis=0) for d in range(2)],
        "lam": [lru_lambda[l, d][None, :] for d in range(2)],
        "g_attn_out": g_attn_out[l][None, :],
        "g_lru_out": g_lru_out[l][None, :],
        "w_out_a": w_out_b[:ATTN_WIDTH],
        "w_out_l": w_out_b[ATTN_WIDTH:],
        "g_ffn": g_ffn[l][None, :],
        "w_query": w_query[l].astype(BF16),
        "keys_pad": _padded_keys(sub_keys[l]),
        "e_down": expert_down[l].astype(BF16),
        "e_up": expert_up[l].astype(BF16),
    }


def kernel(x_prompt, x_sample, g_mix, w_in, g_q, g_k, conv_w, conv_b, w_gate_a, b_gate_a, w_gate_x, b_gate_x, lru_lambda, g_attn_out, g_lru_out, w_out, g_ffn, w_query, sub_keys, expert_down, expert_up):
    weights = (g_mix, w_in, g_q, g_k, conv_w, conv_b, w_gate_a, b_gate_a, w_gate_x, b_gate_x, lru_lambda,
               g_attn_out, g_lru_out, w_out, g_ffn, w_query, sub_keys, expert_down, expert_up)
    layers = [_layer_params(l, *weights) for l in range(w_in.shape[0])]
    return (_trunk(x_prompt, layers), _trunk(x_sample, layers))
```

```python
import functools
import math

import jax
import jax.numpy as jnp
from jax import lax
from jax.experimental import pallas as pl
from jax.experimental.pallas import tpu as pltpu

F32 = jnp.float32
BF16 = jnp.bfloat16

D_MODEL = 2048
GRID_W = 64
N_HEADS = 8
N_KV_HEADS = 2
HEAD_DIM = 128
HEADS_PER_KV = N_HEADS // N_KV_HEADS
ATTN_WIDTH = N_HEADS * HEAD_DIM
KV_WIDTH = N_KV_HEADS * HEAD_DIM
QKV_WIDTH = ATTN_WIDTH + 2 * KV_WIDTH
ROPE_THETA = 10000.0
LRU_WIDTH = D_MODEL - ATTN_WIDTH
LRU_BLOCK_W = 64
CONV_W = 4
CONV_PAD_LEFT = 2
LRU_C = 8.0
N_KEYS = 128
N_EXPERTS = N_KEYS * N_KEYS
PEER_HEADS = 8
PEER_KEY_DIM = 128
PEER_HALF = PEER_KEY_DIM // 2
PEER_TOPK = 16
EPS = 1e-6

V7X_SUBLANES = 8
V7X_LANES = 128
V7X_MXU_DIM = 256
V7X_VMEM_LIMIT_BYTES = 56 * 1024 * 1024

NEG_INF = float("-inf")


def _params(*semantics, flags=None):
    return pltpu.CompilerParams(dimension_semantics=semantics,
                                vmem_limit_bytes=V7X_VMEM_LIMIT_BYTES,
                                flags=flags)


def _block(n, target):
    b = min(n, target)
    while n % b:
        b //= 2
    return b


def _gelu(x):
    return 0.5 * x * (1.0 + lax.erf(x * (1.0 / math.sqrt(2.0))))


def _sigmoid(x):
    return 0.5 + 0.5 * jnp.tanh(0.5 * x)


def _rms(x, g):
    return x * lax.rsqrt(jnp.mean(x * x, axis=-1, keepdims=True) + EPS) * g


def _qkv_kernel(x_ref, gmix_ref, w_ref, gq_ref, gk_ref, cos_ref, sin_ref,
                q_ref, k_ref, v_ref):
    tm = x_ref.shape[0]
    n_sub = 2 if tm % (4 * V7X_SUBLANES) == 0 else 1
    halves = [slice(c * (tm // n_sub), (c + 1) * (tm // n_sub)) for c in range(n_sub)]
    zs = [jnp.dot(_rms(x_ref[rows, :], gmix_ref[...]).astype(BF16), w_ref[...], preferred_element_type=F32)
          for rows in halves]
    lane = lax.broadcasted_iota(jnp.int32, (tm // n_sub, HEAD_DIM), 1)
    first_half = (lane % (HEAD_DIM // 2)) < (HEAD_DIM // 4)
    scale = HEAD_DIM ** -0.5 * math.log2(math.e)
    gq = gq_ref[...]
    gk = gk_ref[...]
    for rows, z in zip(halves, zs):
        cos = cos_ref[rows, :]
        sin = sin_ref[rows, :]

        def norm_rope(zh, g):
            y = _rms(zh, g)
            partner = jnp.where(first_half,
                                pltpu.roll(y, HEAD_DIM - HEAD_DIM // 4, 1),
                                pltpu.roll(y, HEAD_DIM // 4, 1))
            return y * cos + partner * sin

        for hd in range(N_HEADS):
            sl = slice(hd * HEAD_DIM, (hd + 1) * HEAD_DIM)
            q_ref[rows, sl] = (norm_rope(z[:, sl], gq) * scale).astype(BF16)
        for hd in range(N_KV_HEADS):
            sl = slice(hd * HEAD_DIM, (hd + 1) * HEAD_DIM)
            k_ref[rows, sl] = norm_rope(z[:, ATTN_WIDTH + hd * HEAD_DIM:ATTN_WIDTH + (hd + 1) * HEAD_DIM], gk).astype(BF16)
        ones = jnp.ones((z.shape[0], HEAD_DIM), BF16)
        for hd in range(N_KV_HEADS):
            v0 = ATTN_WIDTH + KV_WIDTH + hd * HEAD_DIM
            v_ref[rows, 2 * hd * HEAD_DIM:(2 * hd + 1) * HEAD_DIM] = z[:, v0:v0 + HEAD_DIM].astype(BF16)
            v_ref[rows, (2 * hd + 1) * HEAD_DIM:(2 * hd + 2) * HEAD_DIM] = ones


def _qkv_proj(x2, g_mix, w_qkv, g_q, g_k, cos_t, sin_t, seq):
    n = x2.shape[0]
    tm = _block(seq, 512)
    nseq_blocks = seq // tm
    row = lambda i: (i, 0)
    fixed = lambda i: (0, 0)
    pos = lambda i: (i % nseq_blocks, 0)
    return pl.pallas_call(
        _qkv_kernel,
        grid=(n // tm,),
        in_specs=[
            pl.BlockSpec((tm, D_MODEL), row),
            pl.BlockSpec((1, D_MODEL), fixed),
            pl.BlockSpec((D_MODEL, QKV_WIDTH), fixed),
            pl.BlockSpec((1, HEAD_DIM), fixed),
            pl.BlockSpec((1, HEAD_DIM), fixed),
            pl.BlockSpec((tm, HEAD_DIM), pos),
            pl.BlockSpec((tm, HEAD_DIM), pos),
        ],
        out_specs=[
            pl.BlockSpec((tm, ATTN_WIDTH), row),
            pl.BlockSpec((tm, KV_WIDTH), row),
            pl.BlockSpec((tm, 2 * KV_WIDTH), row),
        ],
        out_shape=[
            jax.ShapeDtypeStruct((n, ATTN_WIDTH), BF16),
            jax.ShapeDtypeStruct((n, KV_WIDTH), BF16),
            jax.ShapeDtypeStruct((n, 2 * KV_WIDTH), BF16),
        ],
        compiler_params=_params("parallel"),
        name="qkv_proj",
    )(x2, g_mix, w_qkv, g_q, g_k, cos_t, sin_t)


def _xy_kernel(x_ref, gmix_ref, w_ref, xr_ref, yr_ref):
    h = _rms(x_ref[...], gmix_ref[...]).astype(BF16)
    z = jnp.dot(h, w_ref[...], preferred_element_type=F32)
    xr_ref[...] = z[:, :LRU_WIDTH]
    yr_ref[...] = z[:, LRU_WIDTH:]


def _xy_proj(x2, g_mix, w_xy):
    n = x2.shape[0]
    tm = _block(n, 512)
    row = lambda i: (i, 0)
    fixed = lambda i: (0, 0)
    return pl.pallas_call(
        _xy_kernel,
        grid=(n // tm,),
        in_specs=[
            pl.BlockSpec((tm, D_MODEL), row),
            pl.BlockSpec((1, D_MODEL), fixed),
            pl.BlockSpec((D_MODEL, 2 * LRU_WIDTH), fixed),
        ],
        out_specs=[pl.BlockSpec((tm, LRU_WIDTH), row), pl.BlockSpec((tm, LRU_WIDTH), row)],
        out_shape=[jax.ShapeDtypeStruct((n, LRU_WIDTH), F32)] * 2,
        compiler_params=_params("parallel"),
        name="xy_proj",
    )(x2, g_mix, w_xy)


def _attn_kernel(tk, q_ref, k_ref, v_ref, o_ref, m_scr, acc_scr):
    m_scr[...] = jnp.full(m_scr.shape, NEG_INF, F32)
    acc_scr[...] = jnp.zeros(acc_scr.shape, F32)
    tq = q_ref.shape[0]
    lane_tiles = tk // HEAD_DIM

    def kv_block(ki, carry):
        start = pl.multiple_of(ki * tk, tk)
        _attn_block(q_ref, k_ref[pl.ds(start, tk), :], v_ref[pl.ds(start, tk), :], m_scr, acc_scr, tq, tk, lane_tiles)
        return carry

    lax.fori_loop(0, k_ref.shape[0] // tk, kv_block, 0)
    for hd in range(HEADS_PER_KV):
        acc = acc_scr[hd]
        o_ref[:, hd * HEAD_DIM:(hd + 1) * HEAD_DIM] = acc[:, :HEAD_DIM] / acc[:, HEAD_DIM:]


def _attn_block(q_ref, k, v, m_scr, acc_scr, tq, tk, lane_tiles):
    nt = (((1,), (1,)), ((), ()))
    for hd in range(HEADS_PER_KV):
        q = q_ref[:, hd * HEAD_DIM:(hd + 1) * HEAD_DIM]
        if hd == 0 and tk % (2 * V7X_MXU_DIM) == 0:
            s = jnp.concatenate([lax.dot_general(q, k[:tk // 2], nt, preferred_element_type=F32),
                                 lax.dot_general(q, k[tk // 2:], nt, preferred_element_type=F32)], axis=1)
        else:
            s = lax.dot_general(q, k, nt, preferred_element_type=F32)
        m_prev = m_scr[hd]
        m_next = jnp.maximum(m_prev, jnp.max(s, axis=1, keepdims=True))
        p = jnp.exp2(s - jnp.tile(m_next, (1, lane_tiles))).astype(BF16)
        alpha = jnp.exp2(m_prev - m_next)
        if hd == HEADS_PER_KV - 1 and tq % 32 == 0:
            pv = jnp.concatenate([jnp.dot(p[:tq // 2], v, preferred_element_type=F32),
                                  jnp.dot(p[tq // 2:], v, preferred_element_type=F32)], axis=0)
        else:
            pv = jnp.dot(p, v, preferred_element_type=F32)
        acc_scr[hd] = jnp.tile(alpha, (1, 2)) * acc_scr[hd] + pv
        m_scr[hd] = m_next


def _attention(q, k, v, batch, seq):
    n = q.shape[0]
    tq = _block(seq, 1024)
    tk = _block(seq, 2048)
    nq = seq // tq
    group_w = HEADS_PER_KV * HEAD_DIM
    return pl.pallas_call(
        functools.partial(_attn_kernel, tk),
        grid=(batch, N_KV_HEADS, nq),
        in_specs=[
            pl.BlockSpec((tq, group_w), lambda b, g, qi: (b * nq + qi, g)),
            pl.BlockSpec((seq, HEAD_DIM), lambda b, g, qi: (b, g)),
            pl.BlockSpec((seq, 2 * HEAD_DIM), lambda b, g, qi: (b, g)),
        ],
        out_specs=pl.BlockSpec((tq, group_w), lambda b, g, qi: (b * nq + qi, g)),
        out_shape=jax.ShapeDtypeStruct((n, ATTN_WIDTH), F32),
        scratch_shapes=[
            pltpu.VMEM((HEADS_PER_KV, tq, HEAD_DIM), F32),
            pltpu.VMEM((HEADS_PER_KV, tq, 2 * HEAD_DIM), F32),
        ],
        compiler_params=_params("parallel", "parallel", "parallel"),
        name="flash_attention",
    )(q, k, v)


def _lru_kernel(reverse, tc, n_chunks, *refs):
    if reverse:
        (xc_ref, wg_ref, bg_ref, lam_ref, hf_ref, yr_ref, out_ref, a_scr, u_scr, carry_scr) = refs
    else:
        (xm_ref, xp_ref, xn_ref, cw_ref, cb_ref, wg_ref, bg_ref, lam_ref,
         out_ref, xc_out_ref, ext_scr, a_scr, u_scr, carry_scr) = refs
    c = pl.program_id(1)

    @pl.when(c == 0)
    def _():
        carry_scr[...] = jnp.zeros(carry_scr.shape, F32)

    if reverse:
        xc = xc_ref[...]
    else:
        halo = V7X_SUBLANES
        ext_scr[0:halo, :] = jnp.where(c == 0, 0.0, xp_ref[...])
        ext_scr[halo:halo + tc, :] = xm_ref[...]
        ext_scr[halo + tc:2 * halo + tc, :] = jnp.where(c == n_chunks - 1, 0.0, xn_ref[...])
        xc = cb_ref[...]
        for tap in range(CONV_W):
            start = halo + tap - CONV_PAD_LEFT
            xc = xc + ext_scr[start:start + tc, :] * cw_ref[tap:tap + 1, :]
        xc_out_ref[...] = xc

    xcb = xc.astype(BF16)
    za, zx = [], []
    for g in range(LRU_WIDTH // V7X_MXU_DIM):
        zg = jnp.dot(xcb[:, g * V7X_MXU_DIM:(g + 1) * V7X_MXU_DIM], wg_ref[g],
                     preferred_element_type=F32)
        za.append(zg[:, :V7X_MXU_DIM])
        zx.append(zg[:, V7X_MXU_DIM:])
    r = _sigmoid(jnp.concatenate(za, axis=1) + bg_ref[0:1, :])
    i = _sigmoid(jnp.concatenate(zx, axis=1) + bg_ref[1:2, :])
    lam = lam_ref[...]
    softplus_neg = jnp.maximum(-lam, 0.0) + jnp.log(1.0 + jnp.exp(-jnp.abs(lam)))
    a = jnp.exp((-LRU_C * softplus_neg) * r)
    u = jnp.sqrt(1.0 - a * a) * (i * xc)

    nt = tc // V7X_SUBLANES
    a3 = a.reshape(nt, V7X_SUBLANES, LRU_WIDTH)
    u3 = u.reshape(nt, V7X_SUBLANES, LRU_WIDTH)
    row = lax.broadcasted_iota(jnp.int32, a3.shape, 1)
    for s in (1, 2, 4):
        if reverse:
            shift, valid = V7X_SUBLANES - s, row < V7X_SUBLANES - s
        else:
            shift, valid = s, row >= s
        a_sh = pltpu.roll(a3, shift, 1)
        u_sh = pltpu.roll(u3, shift, 1)
        u3 = u3 + jnp.where(valid, a3 * u_sh, 0.0)
        a3 = jnp.where(valid, a3 * a_sh, a3)
    a_scr[...] = a3
    u_scr[...] = u3

    def tile_step(t, carry):
        tt = (nt - 1 - t) if reverse else t
        h = u_scr[tt] + a_scr[tt] * carry
        u_scr[tt] = h
        edge = h[0:1, :] if reverse else h[V7X_SUBLANES - 1:V7X_SUBLANES, :]
        return jnp.broadcast_to(edge, h.shape)

    carry_scr[...] = lax.fori_loop(0, nt, tile_step, carry_scr[...])
    h_all = u_scr[...].reshape(tc, LRU_WIDTH)
    if reverse:
        out_ref[...] = (h_all + hf_ref[...]) * _gelu(yr_ref[...])
    else:
        out_ref[...] = h_all


def _lru_direction(reverse, x_in, conv_w, conv_b, w_gates, b_gates, lam, batch, seq, h_fwd=None, yr=None):
    n = x_in.shape[0]
    tc = _block(seq, 1024)
    n_chunks = seq // tc
    halo = V7X_SUBLANES
    per_tile = tc // halo
    n_tiles = n // halo

    def chunk_of(c):
        return (n_chunks - 1 - c) if reverse else c

    main = lambda b, c: (b * n_chunks + chunk_of(c), 0)
    prev = lambda b, c: (jnp.maximum((b * n_chunks + c) * per_tile - 1, 0), 0)
    nxt = lambda b, c: (jnp.minimum((b * n_chunks + c + 1) * per_tile, n_tiles - 1), 0)
    fixed2 = lambda b, c: (0, 0)
    fixed3 = lambda b, c: (0, 0, 0)
    block = pl.BlockSpec((tc, LRU_WIDTH), main)
    gate_specs = [
        pl.BlockSpec((LRU_WIDTH // V7X_MXU_DIM, V7X_MXU_DIM, 2 * V7X_MXU_DIM), fixed3),
        pl.BlockSpec((2, LRU_WIDTH), fixed2),
        pl.BlockSpec((1, LRU_WIDTH), fixed2),
    ]
    scan_scratch = [
        pltpu.VMEM((per_tile, halo, LRU_WIDTH), F32),
        pltpu.VMEM((per_tile, halo, LRU_WIDTH), F32),
        pltpu.VMEM((halo, LRU_WIDTH), F32),
    ]
    out_sds = jax.ShapeDtypeStruct((n, LRU_WIDTH), F32)
    if reverse:
        in_specs = [block] + gate_specs + [block, block]
        args = [x_in, w_gates, b_gates, lam, h_fwd, yr]
        out_specs, out_shape, scratch = block, out_sds, scan_scratch
    else:
        in_specs = [block, pl.BlockSpec((halo, LRU_WIDTH), prev), pl.BlockSpec((halo, LRU_WIDTH), nxt),
                    pl.BlockSpec((CONV_W, LRU_WIDTH), fixed2), pl.BlockSpec((1, LRU_WIDTH), fixed2)] + gate_specs
        args = [x_in, x_in, x_in, conv_w, conv_b, w_gates, b_gates, lam]
        out_specs, out_shape = [block, block], [out_sds, out_sds]
        scratch = [pltpu.VMEM((tc + 2 * halo, LRU_WIDTH), F32)] + scan_scratch
    return pl.pallas_call(
        functools.partial(_lru_kernel, reverse, tc, n_chunks),
        grid=(batch, n_chunks),
        in_specs=in_specs,
        out_specs=out_specs,
        out_shape=out_shape,
        scratch_shapes=scratch,
        compiler_params=_params("parallel", "arbitrary"),
        name="lru_bwd" if reverse else "lru_fwd",
    )(*args)


def _out_kernel(x_ref, attn_ref, lru_ref, ga_ref, gl_ref, wa_ref, wl_ref, gf_ref, wq_ref,
                x1_ref, xnt_ref, qp_ref):
    an = _rms(attn_ref[...], ga_ref[...]).astype(BF16)
    ln = _rms(lru_ref[...], gl_ref[...]).astype(BF16)
    x1 = (x_ref[...] + jnp.dot(an, wa_ref[...], preferred_element_type=F32)
          + jnp.dot(ln, wl_ref[...], preferred_element_type=F32))
    x1_ref[...] = x1
    xn = _rms(x1, gf_ref[...])
    xnt_ref[...] = xn.T.astype(BF16)
    qp_ref[...] = jnp.dot(xn.astype(BF16), wq_ref[...], preferred_element_type=F32)


def _out_proj(x2, attn, lru, g_attn, g_lru, w_out_a, w_out_l, g_ffn, w_query):
    n = x2.shape[0]
    tm = _block(n, 512)
    qw = PEER_HEADS * PEER_KEY_DIM
    row = lambda i: (i, 0)
    fixed = lambda i: (0, 0)
    return pl.pallas_call(
        _out_kernel,
        grid=(n // tm,),
        in_specs=[
            pl.BlockSpec((tm, D_MODEL), row),
            pl.BlockSpec((tm, ATTN_WIDTH), row),
            pl.BlockSpec((tm, LRU_WIDTH), row),
            pl.BlockSpec((1, ATTN_WIDTH), fixed),
            pl.BlockSpec((1, LRU_WIDTH), fixed),
            pl.BlockSpec((ATTN_WIDTH, D_MODEL), fixed),
            pl.BlockSpec((LRU_WIDTH, D_MODEL), fixed),
            pl.BlockSpec((1, D_MODEL), fixed),
            pl.BlockSpec((D_MODEL, qw), fixed),
        ],
        out_specs=[
            pl.BlockSpec((tm, D_MODEL), row),
            pl.BlockSpec((D_MODEL, tm), lambda i: (0, i)),
            pl.BlockSpec((tm, qw), row),
        ],
        out_shape=[
            jax.ShapeDtypeStruct((n, D_MODEL), F32),
            jax.ShapeDtypeStruct((D_MODEL, n), BF16),
            jax.ShapeDtypeStruct((n, qw), F32),
        ],
        compiler_params=_params("parallel"),
        name="out_proj",
    )(x2, attn, lru, g_attn, g_lru, w_out_a, w_out_l, g_ffn, w_query)


def _oddeven_merge(lo, hi, r):
    step = r * 2
    if step < hi - lo:
        yield from _oddeven_merge(lo, hi, step)
        yield from _oddeven_merge(lo + r, hi, step)
        yield from [(i, i + r) for i in range(lo + r, hi - r, step)]
    else:
        yield (lo, lo + r)


def _oddeven_merge_sort(lo, hi):
    if hi - lo >= 1:
        mid = lo + (hi - lo) // 2
        yield from _oddeven_merge_sort(lo, mid)
        yield from _oddeven_merge_sort(mid + 1, hi)
        yield from _oddeven_merge(lo, hi, 1)


def _pop_lists(lists, count):
    lists = list(lists)
    vals = []
    for it in range(count):
        m = jnp.max(lists[0], axis=0, keepdims=True)
        vals.append(m)
        hit = lists[0] == m
        last = min(len(lists), count - it) - 1
        if it < count - 1:
            for c in range(last):
                lists[c] = jnp.where(hit, lists[c + 1], lists[c])
            lists[last] = jnp.where(hit, NEG_INF, lists[last])
    return vals


def _top_values(s, count):
    tiles = [s[v * V7X_SUBLANES:(v + 1) * V7X_SUBLANES, :] for v in range(s.shape[0] // V7X_SUBLANES)]
    for i, j in _oddeven_merge_sort(0, len(tiles) - 1):
        tiles[i], tiles[j] = jnp.maximum(tiles[i], tiles[j]), jnp.minimum(tiles[i], tiles[j])
    return _pop_lists(tiles, count)


def _peer_score_kernel(qp_ref, keys_ref, n1_ref, e1_ref, c2_ref, e2_ref):
    half_tile = V7X_SUBLANES
    sub = lax.broadcasted_iota(jnp.int32, (half_tile, qp_ref.shape[0]), 0)
    for hd in range(PEER_HEADS):
        qh = qp_ref[:, hd * PEER_KEY_DIM:(hd + 1) * PEER_KEY_DIM].astype(BF16)
        nt = (((1,), (1,)), ((), ()))
        s1 = lax.dot_general(keys_ref[hd, 0], qh, nt, preferred_element_type=F32)
        s2 = lax.dot_general(keys_ref[hd, 1], qh, nt, preferred_element_type=F32)
        top1 = _top_values(s1, PEER_TOPK)
        top2 = _top_values(s2, PEER_TOPK)
        a_lo = jnp.concatenate(top1[:half_tile], axis=0)
        a_hi = jnp.concatenate(top1[half_tile:], axis=0)
        lists = []
        for c in range(PEER_TOPK):
            keep = PEER_TOPK // (c + 1)
            row_sums = a_lo + top2[c]
            lists.append(row_sums if keep >= half_tile else jnp.where(sub < keep, row_sums, NEG_INF))
        heads_hi = a_hi + top2[0]
        m12 = top1[0] + top2[0]
        n_lo = jnp.zeros_like(a_lo)
        n_hi = jnp.zeros_like(a_lo)
        z_lo = jnp.zeros_like(a_lo)
        z_hi = jnp.zeros_like(a_lo)
        for it in range(PEER_TOPK):
            m = jnp.maximum(jnp.max(lists[0], axis=0, keepdims=True), jnp.max(heads_hi, axis=0, keepdims=True))
            e = jnp.exp(m - m12)
            hit_lo = lists[0] == m
            hit_hi = heads_hi == m
            n_lo = n_lo + jnp.where(hit_lo, 1.0, 0.0)
            n_hi = n_hi + jnp.where(hit_hi, 1.0, 0.0)
            z_lo = z_lo + jnp.where(hit_lo, e, 0.0)
            z_hi = z_hi + jnp.where(hit_hi, e, 0.0)
            if it < PEER_TOPK - 1:
                last = PEER_TOPK - it - 1
                for c in range(last):
                    lists[c] = jnp.where(hit_lo, lists[c + 1], lists[c])
                lists[last] = jnp.where(hit_lo, NEG_INF, lists[last])
                heads_hi = jnp.where(hit_hi, NEG_INF, heads_hi)
        z = jnp.sum(z_lo + z_hi, axis=0, keepdims=True)
        n1 = jnp.zeros_like(s1)
        rank2 = jnp.zeros_like(s2)
        for r in range(PEER_TOPK):
            n_r = (n_lo if r < half_tile else n_hi)[r % half_tile:r % half_tile + 1, :]
            n1 = jnp.where(s1 == top1[r], n_r, n1)
            rank2 = rank2 + jnp.where(s2 < top2[r], 1.0, 0.0)
        n1_ref[hd] = n1
        e1_ref[hd] = jnp.exp(s1 - top1[0]) / z
        c2_ref[hd] = rank2.astype(BF16)
        e2_ref[hd] = jnp.exp(s2 - top2[0]).astype(BF16)


def _peer_scores(qp, keys_pad):
    n = qp.shape[0]
    tb = _block(n, 512)
    qw = PEER_HEADS * PEER_KEY_DIM
    big = pl.BlockSpec((PEER_HEADS, N_KEYS, tb), lambda i: (0, 0, i))
    shape_f32 = jax.ShapeDtypeStruct((PEER_HEADS, N_KEYS, n), F32)
    shape_bf16 = jax.ShapeDtypeStruct((PEER_HEADS, N_KEYS, n), BF16)
    return pl.pallas_call(
        _peer_score_kernel,
        grid=(n // tb,),
        in_specs=[
            pl.BlockSpec((tb, qw), lambda i: (i, 0)),
            pl.BlockSpec((PEER_HEADS, 2, N_KEYS, PEER_KEY_DIM), lambda i: (0, 0, 0, 0)),
        ],
        out_specs=[big, big, big, big],
        out_shape=[shape_f32, shape_f32, shape_bf16, shape_bf16],
        compiler_params=_params("parallel"),
        name="peer_scores",
    )(qp, keys_pad)


PEER_KEYS_PER_STEP = V7X_SUBLANES
PEER_EXPERTS_PER_STEP = PEER_KEYS_PER_STEP * N_KEYS


def _peer_gate(n1_ref, e1_ref, c2_ref, e2_ref, g_scr):
    tb = g_scr.shape[1]
    bf16_rows = 2 * V7X_SUBLANES

    def key_row(ref, hd, il):
        row = jnp.broadcast_to(ref[hd, il:il + 1, :], (bf16_rows, tb)).astype(BF16)
        return jnp.tile(row, (N_KEYS // bf16_rows, 1))

    for il in range(PEER_KEYS_PER_STEP):
        gate = jnp.zeros((N_KEYS, tb), BF16)
        for hd in range(PEER_HEADS):
            n_row = key_row(n1_ref, hd, il)
            gate = gate + jnp.where(c2_ref[hd] < n_row, e2_ref[hd], 0.0) * key_row(e1_ref, hd, il)
        g_scr[il * N_KEYS:(il + 1) * N_KEYS, :] = gate


PEER_FIRST_MATMULS = 4
PEER_SECOND_SPLIT = 2


def _peer_mix_kernel(xnt_ref, ed_ref, eu_ref, n1_ref, e1_ref, c2_ref, e2_ref, x1_ref, y_ref, g_scr, w_scr):
    ec = pl.program_id(1)

    @pl.when(ec == 0)
    def _():
        y_ref[...] = x1_ref[...]

    _peer_gate(n1_ref, e1_ref, c2_ref, e2_ref, g_scr)
    tn = (((0,), (0,)), ((), ()))
    n_h = PEER_FIRST_MATMULS
    piece = ed_ref.shape[0] // n_h
    pieces = [slice(c * piece, (c + 1) * piece) for c in range(n_h)]
    hids = [jnp.dot(ed_ref[rows, :], xnt_ref[...], preferred_element_type=F32) for rows in pieces]
    parts = []
    for lo, hi in ((0, PEER_SECOND_SPLIT), (PEER_SECOND_SPLIT, n_h)):
        for q in range(lo, hi):
            w_scr[pieces[q], :] = _gelu(hids[q].astype(BF16)) * g_scr[pieces[q], :]
        rows = slice(lo * piece, hi * piece)
        parts.append(lax.dot_general(w_scr[rows, :], eu_ref[rows, :], tn, preferred_element_type=F32))
    y_ref[...] += parts[0] + parts[1]


def _peer_mix(xnt, e_down, e_up, n1, e1, c2, e2, x1):
    n = xnt.shape[1]
    tb = _block(n, 512)
    ec = PEER_EXPERTS_PER_STEP
    tok = lambda t, e: (t, 0)
    exp_ = lambda t, e: (e, 0)
    key1 = pl.BlockSpec((PEER_HEADS, PEER_KEYS_PER_STEP, tb), lambda t, e: (0, e, t))
    key2 = pl.BlockSpec((PEER_HEADS, N_KEYS, tb), lambda t, e: (0, 0, t))
    return pl.pallas_call(
        _peer_mix_kernel,
        grid=(n // tb, N_EXPERTS // ec),
        in_specs=[
            pl.BlockSpec((D_MODEL, tb), lambda t, e: (0, t)),
            pl.BlockSpec((ec, D_MODEL), exp_),
            pl.BlockSpec((ec, D_MODEL), exp_),
            key1, key1, key2, key2,
            pl.BlockSpec((tb, D_MODEL), tok),
        ],
        out_specs=pl.BlockSpec((tb, D_MODEL), tok),
        out_shape=jax.ShapeDtypeStruct((n, D_MODEL), F32),
        scratch_shapes=[pltpu.VMEM((ec, tb), BF16), pltpu.VMEM((ec, tb), BF16)],
        compiler_params=_params("parallel", "arbitrary"),
        name="peer_mix",
    )(xnt, e_down, e_up, n1, e1, c2, e2, x1)


def _rope_tables(seq):
    t = jnp.arange(seq, dtype=jnp.int32)
    row = (t // GRID_W).astype(F32)
    col = (t % GRID_W).astype(F32)
    half = HEAD_DIM // 2
    inv = ROPE_THETA ** (-jnp.arange(0, half, 2, dtype=F32) / half)
    ar = row[:, None] * inv
    ac = col[:, None] * inv
    cos_t = jnp.concatenate([jnp.cos(ar), jnp.cos(ar), jnp.cos(ac), jnp.cos(ac)], axis=-1)
    sin_t = jnp.concatenate([-jnp.sin(ar), jnp.sin(ar), -jnp.sin(ac), jnp.sin(ac)], axis=-1)
    return cos_t, sin_t


def _gate_weights(w_gate_a, w_gate_x):
    per_tile = V7X_MXU_DIM // LRU_BLOCK_W
    n_tiles = LRU_WIDTH // V7X_MXU_DIM

    def tiles(w):
        w = w.reshape(n_tiles, per_tile, LRU_BLOCK_W, LRU_BLOCK_W)
        eye = jnp.eye(per_tile, dtype=w.dtype)
        return jnp.einsum("tpkj,pq->tpkqj", w, eye).reshape(n_tiles, V7X_MXU_DIM, V7X_MXU_DIM)

    return jnp.concatenate([tiles(w_gate_a), tiles(w_gate_x)], axis=-1).astype(BF16)


def _padded_keys(sub_keys):
    z = jnp.zeros_like(sub_keys[:, 0])
    k0 = jnp.concatenate([sub_keys[:, 0], z], axis=-1)
    k1 = jnp.concatenate([z, sub_keys[:, 1]], axis=-1)
    return jnp.stack([k0, k1], axis=1).astype(BF16)


def _layer(x2, p, batch, seq, cos_t, sin_t):
    q, k, v = _qkv_proj(x2, p["g_mix"], p["w_qkv"], p["g_q"], p["g_k"], cos_t, sin_t, seq)
    attn = _attention(q, k, v, batch, seq)
    xr, yr = _xy_proj(x2, p["g_mix"], p["w_xy"])
    h_fwd, xc = _lru_direction(False, xr, p["conv_w"], p["conv_b"], p["w_gates"][0], p["b_gates"][0],
                               p["lam"][0], batch, seq)
    lru = _lru_direction(True, xc, p["conv_w"], p["conv_b"], p["w_gates"][1], p["b_gates"][1],
                         p["lam"][1], batch, seq, h_fwd=h_fwd, yr=yr)
    x1, xnt, qp = _out_proj(x2, attn, lru, p["g_attn_out"], p["g_lru_out"], p["w_out_a"], p["w_out_l"],
                           p["g_ffn"], p["w_query"])
    n1, e1, c2, e2 = _peer_scores(qp, p["keys_pad"])
    return _peer_mix(xnt, p["e_down"], p["e_up"], n1, e1, c2, e2, x1)


def _trunk(x, layers):
    batch, seq, _ = x.shape
    x2 = x.reshape(batch * seq, D_MODEL)
    cos_t, sin_t = _rope_tables(seq)
    for p in layers:
        x2 = _layer(x2, p, batch, seq, cos_t, sin_t)
    return x2.reshape(batch, seq, D_MODEL)


def _layer_params(l, g_mix, w_in, g_q, g_k, conv_w, conv_b, w_gate_a, b_gate_a, w_gate_x, b_gate_x, lru_lambda,
                  g_attn_out, g_lru_out, w_out, g_ffn, w_query, sub_keys, expert_down, expert_up):
    w_in_b = w_in[l].astype(BF16)
    w_out_b = w_out[l].astype(BF16)
    return {
        "g_mix": g_mix[l][None, :],
        "w_qkv": w_in_b[:, :QKV_WIDTH],
        "w_xy": w_in_b[:, QKV_WIDTH:],
        "g_q": g_q[l][None, :],
        "g_k": g_k[l][None, :],
        "conv_w": conv_w[l],
        "conv_b": conv_b[l][None, :],
        "w_gates": [_gate_weights(w_gate_a[l, d], w_gate_x[l, d]) for d in range(2)],
        "b_gates": [jnp.stack([b_gate_a[l, d], b_gate_x[l, d]], axis=0) for d in range(2)],
        "lam": [lru_lambda[l, d][None, :] for d in range(2)],
        "g_attn_out": g_attn_out[l][None, :],
        "g_lru_out": g_lru_out[l][None, :],
        "w_out_a": w_out_b[:ATTN_WIDTH],
        "w_out_l": w_out_b[ATTN_WIDTH:],
        "g_ffn": g_ffn[l][None, :],
        "w_query": w_query[l].astype(BF16),
        "keys_pad": _padded_keys(sub_keys[l]),
        "e_down": expert_down[l].astype(BF16),
        "e_up": expert_up[l].astype(BF16),
    }


def kernel(x_prompt, x_sample, g_mix, w_in, g_q, g_k, conv_w, conv_b, w_gate_a, b_gate_a, w_gate_x, b_gate_x, lru_lambda, g_attn_out, g_lru_out, w_out, g_ffn, w_query, sub_keys, expert_down, expert_up):
    weights = (g_mix, w_in, g_q, g_k, conv_w, conv_b, w_gate_a, b_gate_a, w_gate_x, b_gate_x, lru_lambda,
               g_attn_out, g_lru_out, w_out, g_ffn, w_query, sub_keys, expert_down, expert_up)
    layers = [_layer_params(l, *weights) for l in range(w_in.shape[0])]
    return (_trunk(x_prompt, layers), _trunk(x_sample, layers))
```

```python
import functools
import math

import jax
import jax.numpy as jnp
from jax import lax
from jax.experimental import pallas as pl
from jax.experimental.pallas import tpu as pltpu

F32 = jnp.float32
BF16 = jnp.bfloat16

D_MODEL = 2048
GRID_W = 64
N_HEADS = 8
N_KV_HEADS = 2
HEAD_DIM = 128
HEADS_PER_KV = N_HEADS // N_KV_HEADS
ATTN_WIDTH = N_HEADS * HEAD_DIM
KV_WIDTH = N_KV_HEADS * HEAD_DIM
QKV_WIDTH = ATTN_WIDTH + 2 * KV_WIDTH
ROPE_THETA = 10000.0
LRU_WIDTH = D_MODEL - ATTN_WIDTH
LRU_BLOCK_W = 64
CONV_W = 4
CONV_PAD_LEFT = 2
LRU_C = 8.0
N_KEYS = 128
N_EXPERTS = N_KEYS * N_KEYS
PEER_HEADS = 8
PEER_KEY_DIM = 128
PEER_HALF = PEER_KEY_DIM // 2
PEER_TOPK = 16
EPS = 1e-6

V7X_SUBLANES = 8
V7X_LANES = 128
V7X_MXU_DIM = 256
V7X_VMEM_LIMIT_BYTES = 56 * 1024 * 1024

NEG_INF = float("-inf")


def _params(*semantics, flags=None):
    return pltpu.CompilerParams(dimension_semantics=semantics,
                                vmem_limit_bytes=V7X_VMEM_LIMIT_BYTES,
                                flags=flags)


def _block(n, target):
    b = min(n, target)
    while n % b:
        b //= 2
    return b


def _gelu(x):
    return 0.5 * x * (1.0 + lax.erf(x * (1.0 / math.sqrt(2.0))))


def _sigmoid(x):
    return 0.5 + 0.5 * jnp.tanh(0.5 * x)


def _rms(x, g):
    return x * lax.rsqrt(jnp.mean(x * x, axis=-1, keepdims=True) + EPS) * g


def _qkv_kernel(x_ref, gmix_ref, w_ref, gq_ref, gk_ref, cos_ref, sin_ref,
                q_ref, k_ref, v_ref):
    tm = x_ref.shape[0]
    n_sub = 2 if tm % (4 * V7X_SUBLANES) == 0 else 1
    halves = [slice(c * (tm // n_sub), (c + 1) * (tm // n_sub)) for c in range(n_sub)]
    zs = [jnp.dot(_rms(x_ref[rows, :], gmix_ref[...]).astype(BF16), w_ref[...], preferred_element_type=F32)
          for rows in halves]
    lane = lax.broadcasted_iota(jnp.int32, (tm // n_sub, HEAD_DIM), 1)
    first_half = (lane % (HEAD_DIM // 2)) < (HEAD_DIM // 4)
    scale = HEAD_DIM ** -0.5 * math.log2(math.e)
    gq = gq_ref[...]
    gk = gk_ref[...]
    for rows, z in zip(halves, zs):
        cos = cos_ref[rows, :]
        sin = sin_ref[rows, :]

        def norm_rope(zh, g):
            y = _rms(zh, g)
            partner = jnp.where(first_half,
                                pltpu.roll(y, HEAD_DIM - HEAD_DIM // 4, 1),
                                pltpu.roll(y, HEAD_DIM // 4, 1))
            return y * cos + partner * sin

        for hd in range(N_HEADS):
            sl = slice(hd * HEAD_DIM, (hd + 1) * HEAD_DIM)
            q_ref[rows, sl] = (norm_rope(z[:, sl], gq) * scale).astype(BF16)
        for hd in range(N_KV_HEADS):
            sl = slice(hd * HEAD_DIM, (hd + 1) * HEAD_DIM)
            k_ref[rows, sl] = norm_rope(z[:, ATTN_WIDTH + hd * HEAD_DIM:ATTN_WIDTH + (hd + 1) * HEAD_DIM], gk).astype(BF16)
        ones = jnp.ones((z.shape[0], HEAD_DIM), BF16)
        for hd in range(N_KV_HEADS):
            v0 = ATTN_WIDTH + KV_WIDTH + hd * HEAD_DIM
            v_ref[rows, 2 * hd * HEAD_DIM:(2 * hd + 1) * HEAD_DIM] = z[:, v0:v0 + HEAD_DIM].astype(BF16)
            v_ref[rows, (2 * hd + 1) * HEAD_DIM:(2 * hd + 2) * HEAD_DIM] = ones


def _qkv_proj(x2, g_mix, w_qkv, g_q, g_k, cos_t, sin_t, seq):
    n = x2.shape[0]
    tm = _block(seq, 512)
    nseq_blocks = seq // tm
    row = lambda i: (i, 0)
    fixed = lambda i: (0, 0)
    pos = lambda i: (i % nseq_blocks, 0)
    return pl.pallas_call(
        _qkv_kernel,
        grid=(n // tm,),
        in_specs=[
            pl.BlockSpec((tm, D_MODEL), row),
            pl.BlockSpec((1, D_MODEL), fixed),
            pl.BlockSpec((D_MODEL, QKV_WIDTH), fixed),
            pl.BlockSpec((1, HEAD_DIM), fixed),
            pl.BlockSpec((1, HEAD_DIM), fixed),
            pl.BlockSpec((tm, HEAD_DIM), pos),
            pl.BlockSpec((tm, HEAD_DIM), pos),
        ],
        out_specs=[
            pl.BlockSpec((tm, ATTN_WIDTH), row),
            pl.BlockSpec((tm, KV_WIDTH), row),
            pl.BlockSpec((tm, 2 * KV_WIDTH), row),
        ],
        out_shape=[
            jax.ShapeDtypeStruct((n, ATTN_WIDTH), BF16),
            jax.ShapeDtypeStruct((n, KV_WIDTH), BF16),
            jax.ShapeDtypeStruct((n, 2 * KV_WIDTH), BF16),
        ],
        compiler_params=_params("parallel"),
        name="qkv_proj",
    )(x2, g_mix, w_qkv, g_q, g_k, cos_t, sin_t)


def _xy_kernel(x_ref, gmix_ref, w_ref, xr_ref, yr_ref):
    h = _rms(x_ref[...], gmix_ref[...]).astype(BF16)
    z = jnp.dot(h, w_ref[...], preferred_element_type=F32)
    xr_ref[...] = z[:, :LRU_WIDTH]
    yr_ref[...] = z[:, LRU_WIDTH:]


def _xy_proj(x2, g_mix, w_xy):
    n = x2.shape[0]
    tm = _block(n, 512)
    row = lambda i: (i, 0)
    fixed = lambda i: (0, 0)
    return pl.pallas_call(
        _xy_kernel,
        grid=(n // tm,),
        in_specs=[
            pl.BlockSpec((tm, D_MODEL), row),
            pl.BlockSpec((1, D_MODEL), fixed),
            pl.BlockSpec((D_MODEL, 2 * LRU_WIDTH), fixed),
        ],
        out_specs=[pl.BlockSpec((tm, LRU_WIDTH), row), pl.BlockSpec((tm, LRU_WIDTH), row)],
        out_shape=[jax.ShapeDtypeStruct((n, LRU_WIDTH), F32)] * 2,
        compiler_params=_params("parallel"),
        name="xy_proj",
    )(x2, g_mix, w_xy)


def _attn_kernel(tk, q_ref, k_ref, v_ref, o_ref, m_scr, acc_scr):
    m_scr[...] = jnp.full(m_scr.shape, NEG_INF, F32)
    acc_scr[...] = jnp.zeros(acc_scr.shape, F32)
    tq = q_ref.shape[0]
    lane_tiles = tk // HEAD_DIM

    def kv_block(ki, carry):
        start = pl.multiple_of(ki * tk, tk)
        _attn_block(q_ref, k_ref[pl.ds(start, tk), :], v_ref[pl.ds(start, tk), :], m_scr, acc_scr, tq, tk, lane_tiles)
        return carry

    n_blocks = k_ref.shape[0] // tk
    lax.fori_loop(0, n_blocks, kv_block, 0, unroll=2 if n_blocks % 2 == 0 else 1)
    for hd in range(HEADS_PER_KV):
        acc = acc_scr[hd]
        o_ref[:, hd * HEAD_DIM:(hd + 1) * HEAD_DIM] = acc[:, :HEAD_DIM] / acc[:, HEAD_DIM:]


def _attn_block(q_ref, k, v, m_scr, acc_scr, tq, tk, lane_tiles):
    nt = (((1,), (1,)), ((), ()))
    for hd in range(HEADS_PER_KV):
        q = q_ref[:, hd * HEAD_DIM:(hd + 1) * HEAD_DIM]
        if hd == 0 and tk % (2 * V7X_MXU_DIM) == 0:
            s = jnp.concatenate([lax.dot_general(q, k[:tk // 2], nt, preferred_element_type=F32),
                                 lax.dot_general(q, k[tk // 2:], nt, preferred_element_type=F32)], axis=1)
        else:
            s = lax.dot_general(q, k, nt, preferred_element_type=F32)
        m_prev = m_scr[hd]
        m_next = jnp.maximum(m_prev, jnp.max(s, axis=1, keepdims=True))
        p = jnp.exp2(s - jnp.tile(m_next, (1, lane_tiles))).astype(BF16)
        alpha = jnp.exp2(m_prev - m_next)
        if hd == HEADS_PER_KV - 1 and tq % 32 == 0:
            pv = jnp.concatenate([jnp.dot(p[:tq // 2], v, preferred_element_type=F32),
                                  jnp.dot(p[tq // 2:], v, preferred_element_type=F32)], axis=0)
        else:
            pv = jnp.dot(p, v, preferred_element_type=F32)
        acc_scr[hd] = jnp.tile(alpha, (1, 2)) * acc_scr[hd] + pv
        m_scr[hd] = m_next


def _attention(q, k, v, batch, seq):
    n = q.shape[0]
    tq = _block(seq, 1024)
    tk = _block(seq, 2048)
    nq = seq // tq
    group_w = HEADS_PER_KV * HEAD_DIM
    return pl.pallas_call(
        functools.partial(_attn_kernel, tk),
        grid=(batch, N_KV_HEADS, nq),
        in_specs=[
            pl.BlockSpec((tq, group_w), lambda b, g, qi: (b * nq + qi, g)),
            pl.BlockSpec((seq, HEAD_DIM), lambda b, g, qi: (b, g)),
            pl.BlockSpec((seq, 2 * HEAD_DIM), lambda b, g, qi: (b, g)),
        ],
        out_specs=pl.BlockSpec((tq, group_w), lambda b, g, qi: (b * nq + qi, g)),
        out_shape=jax.ShapeDtypeStruct((n, ATTN_WIDTH), F32),
        scratch_shapes=[
            pltpu.VMEM((HEADS_PER_KV, tq, HEAD_DIM), F32),
            pltpu.VMEM((HEADS_PER_KV, tq, 2 * HEAD_DIM), F32),
        ],
        compiler_params=_params("parallel", "parallel", "parallel"),
        name="flash_attention",
    )(q, k, v)


def _lru_kernel(reverse, tc, n_chunks, *refs):
    if reverse:
        (xc_ref, wg_ref, bg_ref, lam_ref, hf_ref, yr_ref, out_ref, a_scr, u_scr, carry_scr) = refs
    else:
        (xm_ref, xp_ref, xn_ref, cw_ref, cb_ref, wg_ref, bg_ref, lam_ref,
         out_ref, xc_out_ref, ext_scr, a_scr, u_scr, carry_scr) = refs
    c = pl.program_id(1)

    @pl.when(c == 0)
    def _():
        carry_scr[...] = jnp.zeros(carry_scr.shape, F32)

    if reverse:
        xc = xc_ref[...]
    else:
        halo = V7X_SUBLANES
        ext_scr[0:halo, :] = jnp.where(c == 0, 0.0, xp_ref[...])
        ext_scr[halo:halo + tc, :] = xm_ref[...]
        ext_scr[halo + tc:2 * halo + tc, :] = jnp.where(c == n_chunks - 1, 0.0, xn_ref[...])
        xc = cb_ref[...]
        for tap in range(CONV_W):
            start = halo + tap - CONV_PAD_LEFT
            xc = xc + ext_scr[start:start + tc, :] * cw_ref[tap:tap + 1, :]
        xc_out_ref[...] = xc

    xcb = xc.astype(BF16)
    za, zx = [], []
    for g in range(LRU_WIDTH // V7X_MXU_DIM):
        zg = jnp.dot(xcb[:, g * V7X_MXU_DIM:(g + 1) * V7X_MXU_DIM], wg_ref[g],
                     preferred_element_type=F32)
        za.append(zg[:, :V7X_MXU_DIM])
        zx.append(zg[:, V7X_MXU_DIM:])
    r = _sigmoid(jnp.concatenate(za, axis=1) + bg_ref[0:1, :])
    i = _sigmoid(jnp.concatenate(zx, axis=1) + bg_ref[1:2, :])
    lam = lam_ref[...]
    softplus_neg = jnp.maximum(-lam, 0.0) + jnp.log(1.0 + jnp.exp(-jnp.abs(lam)))
    a = jnp.exp((-LRU_C * softplus_neg) * r)
    u = jnp.sqrt(1.0 - a * a) * (i * xc)

    nt = tc // V7X_SUBLANES
    a3 = a.reshape(nt, V7X_SUBLANES, LRU_WIDTH)
    u3 = u.reshape(nt, V7X_SUBLANES, LRU_WIDTH)
    row = lax.broadcasted_iota(jnp.int32, a3.shape, 1)
    for s in (1, 2, 4):
        if reverse:
            shift, valid = V7X_SUBLANES - s, row < V7X_SUBLANES - s
        else:
            shift, valid = s, row >= s
        a_sh = pltpu.roll(a3, shift, 1)
        u_sh = pltpu.roll(u3, shift, 1)
        u3 = u3 + jnp.where(valid, a3 * u_sh, 0.0)
        a3 = jnp.where(valid, a3 * a_sh, a3)
    a_scr[...] = a3
    u_scr[...] = u3

    def tile_step(t, carry):
        tt = (nt - 1 - t) if reverse else t
        h = u_scr[tt] + a_scr[tt] * carry
        u_scr[tt] = h
        edge = h[0:1, :] if reverse else h[V7X_SUBLANES - 1:V7X_SUBLANES, :]
        return jnp.broadcast_to(edge, h.shape)

    carry_scr[...] = lax.fori_loop(0, nt, tile_step, carry_scr[...])
    h_all = u_scr[...].reshape(tc, LRU_WIDTH)
    if reverse:
        out_ref[...] = (h_all + hf_ref[...]) * _gelu(yr_ref[...])
    else:
        out_ref[...] = h_all


def _lru_direction(reverse, x_in, conv_w, conv_b, w_gates, b_gates, lam, batch, seq, h_fwd=None, yr=None):
    n = x_in.shape[0]
    tc = _block(seq, 1024)
    n_chunks = seq // tc
    halo = V7X_SUBLANES
    per_tile = tc // halo
    n_tiles = n // halo

    def chunk_of(c):
        return (n_chunks - 1 - c) if reverse else c

    main = lambda b, c: (b * n_chunks + chunk_of(c), 0)
    prev = lambda b, c: (jnp.maximum((b * n_chunks + c) * per_tile - 1, 0), 0)
    nxt = lambda b, c: (jnp.minimum((b * n_chunks + c + 1) * per_tile, n_tiles - 1), 0)
    fixed2 = lambda b, c: (0, 0)
    fixed3 = lambda b, c: (0, 0, 0)
    block = pl.BlockSpec((tc, LRU_WIDTH), main)
    gate_specs = [
        pl.BlockSpec((LRU_WIDTH // V7X_MXU_DIM, V7X_MXU_DIM, 2 * V7X_MXU_DIM), fixed3),
        pl.BlockSpec((2, LRU_WIDTH), fixed2),
        pl.BlockSpec((1, LRU_WIDTH), fixed2),
    ]
    scan_scratch = [
        pltpu.VMEM((per_tile, halo, LRU_WIDTH), F32),
        pltpu.VMEM((per_tile, halo, LRU_WIDTH), F32),
        pltpu.VMEM((halo, LRU_WIDTH), F32),
    ]
    out_sds = jax.ShapeDtypeStruct((n, LRU_WIDTH), F32)
    if reverse:
        in_specs = [block] + gate_specs + [block, block]
        args = [x_in, w_gates, b_gates, lam, h_fwd, yr]
        out_specs, out_shape, scratch = block, out_sds, scan_scratch
    else:
        in_specs = [block, pl.BlockSpec((halo, LRU_WIDTH), prev), pl.BlockSpec((halo, LRU_WIDTH), nxt),
                    pl.BlockSpec((CONV_W, LRU_WIDTH), fixed2), pl.BlockSpec((1, LRU_WIDTH), fixed2)] + gate_specs
        args = [x_in, x_in, x_in, conv_w, conv_b, w_gates, b_gates, lam]
        out_specs, out_shape = [block, block], [out_sds, out_sds]
        scratch = [pltpu.VMEM((tc + 2 * halo, LRU_WIDTH), F32)] + scan_scratch
    return pl.pallas_call(
        functools.partial(_lru_kernel, reverse, tc, n_chunks),
        grid=(batch, n_chunks),
        in_specs=in_specs,
        out_specs=out_specs,
        out_shape=out_shape,
        scratch_shapes=scratch,
        compiler_params=_params("parallel", "arbitrary"),
        name="lru_bwd" if reverse else "lru_fwd",
    )(*args)


def _out_kernel(x_ref, attn_ref, lru_ref, ga_ref, gl_ref, wa_ref, wl_ref, gf_ref, wq_ref,
                x1_ref, xnt_ref, qp_ref):
    an = _rms(attn_ref[...], ga_ref[...]).astype(BF16)
    ln = _rms(lru_ref[...], gl_ref[...]).astype(BF16)
    x1 = (x_ref[...] + jnp.dot(an, wa_ref[...], preferred_element_type=F32)
          + jnp.dot(ln, wl_ref[...], preferred_element_type=F32))
    x1_ref[...] = x1
    xn = _rms(x1, gf_ref[...])
    xnt_ref[...] = xn.T.astype(BF16)
    qp_ref[...] = jnp.dot(xn.astype(BF16), wq_ref[...], preferred_element_type=F32)


def _out_proj(x2, attn, lru, g_attn, g_lru, w_out_a, w_out_l, g_ffn, w_query):
    n = x2.shape[0]
    tm = _block(n, 512)
    qw = PEER_HEADS * PEER_KEY_DIM
    row = lambda i: (i, 0)
    fixed = lambda i: (0, 0)
    return pl.pallas_call(
        _out_kernel,
        grid=(n // tm,),
        in_specs=[
            pl.BlockSpec((tm, D_MODEL), row),
            pl.BlockSpec((tm, ATTN_WIDTH), row),
            pl.BlockSpec((tm, LRU_WIDTH), row),
            pl.BlockSpec((1, ATTN_WIDTH), fixed),
            pl.BlockSpec((1, LRU_WIDTH), fixed),
            pl.BlockSpec((ATTN_WIDTH, D_MODEL), fixed),
            pl.BlockSpec((LRU_WIDTH, D_MODEL), fixed),
            pl.BlockSpec((1, D_MODEL), fixed),
            pl.BlockSpec((D_MODEL, qw), fixed),
        ],
        out_specs=[
            pl.BlockSpec((tm, D_MODEL), row),
            pl.BlockSpec((D_MODEL, tm), lambda i: (0, i)),
            pl.BlockSpec((tm, qw), row),
        ],
        out_shape=[
            jax.ShapeDtypeStruct((n, D_MODEL), F32),
            jax.ShapeDtypeStruct((D_MODEL, n), BF16),
            jax.ShapeDtypeStruct((n, qw), F32),
        ],
        compiler_params=_params("parallel"),
        name="out_proj",
    )(x2, attn, lru, g_attn, g_lru, w_out_a, w_out_l, g_ffn, w_query)


def _oddeven_merge(lo, hi, r):
    step = r * 2
    if step < hi - lo:
        yield from _oddeven_merge(lo, hi, step)
        yield from _oddeven_merge(lo + r, hi, step)
        yield from [(i, i + r) for i in range(lo + r, hi - r, step)]
    else:
        yield (lo, lo + r)


def _oddeven_merge_sort(lo, hi):
    if hi - lo >= 1:
        mid = lo + (hi - lo) // 2
        yield from _oddeven_merge_sort(lo, mid)
        yield from _oddeven_merge_sort(mid + 1, hi)
        yield from _oddeven_merge(lo, hi, 1)


def _pop_lists(lists, count):
    lists = list(lists)
    vals = []
    for it in range(count):
        m = jnp.max(lists[0], axis=0, keepdims=True)
        vals.append(m)
        hit = lists[0] == m
        last = min(len(lists), count - it) - 1
        if it < count - 1:
            for c in range(last):
                lists[c] = jnp.where(hit, lists[c + 1], lists[c])
            lists[last] = jnp.where(hit, NEG_INF, lists[last])
    return vals


def _top_values(s, count):
    tiles = [s[v * V7X_SUBLANES:(v + 1) * V7X_SUBLANES, :] for v in range(s.shape[0] // V7X_SUBLANES)]
    for i, j in _oddeven_merge_sort(0, len(tiles) - 1):
        tiles[i], tiles[j] = jnp.maximum(tiles[i], tiles[j]), jnp.minimum(tiles[i], tiles[j])
    return _pop_lists(tiles, count)


def _peer_score_kernel(qp_ref, keys_ref, n1_ref, e1_ref, c2_ref, e2_ref):
    half_tile = V7X_SUBLANES
    sub = lax.broadcasted_iota(jnp.int32, (half_tile, qp_ref.shape[0]), 0)
    for hd in range(PEER_HEADS):
        qh = qp_ref[:, hd * PEER_KEY_DIM:(hd + 1) * PEER_KEY_DIM].astype(BF16)
        nt = (((1,), (1,)), ((), ()))
        s1 = lax.dot_general(keys_ref[hd, 0], qh, nt, preferred_element_type=F32)
        s2 = lax.dot_general(keys_ref[hd, 1], qh, nt, preferred_element_type=F32)
        top1 = _top_values(s1, PEER_TOPK)
        top2 = _top_values(s2, PEER_TOPK)
        a_lo = jnp.concatenate(top1[:half_tile], axis=0)
        a_hi = jnp.concatenate(top1[half_tile:], axis=0)
        lists = []
        for c in range(PEER_TOPK):
            keep = PEER_TOPK // (c + 1)
            row_sums = a_lo + top2[c]
            lists.append(row_sums if keep >= half_tile else jnp.where(sub < keep, row_sums, NEG_INF))
        heads_hi = a_hi + top2[0]
        m12 = top1[0] + top2[0]
        n_lo = jnp.zeros_like(a_lo)
        n_hi = jnp.zeros_like(a_lo)
        z_lo = jnp.zeros_like(a_lo)
        z_hi = jnp.zeros_like(a_lo)
        for it in range(PEER_TOPK):
            m = jnp.maximum(jnp.max(lists[0], axis=0, keepdims=True), jnp.max(heads_hi, axis=0, keepdims=True))
            e = jnp.exp(m - m12)
            hit_lo = lists[0] == m
            hit_hi = heads_hi == m
            n_lo = n_lo + jnp.where(hit_lo, 1.0, 0.0)
            n_hi = n_hi + jnp.where(hit_hi, 1.0, 0.0)
            z_lo = z_lo + jnp.where(hit_lo, e, 0.0)
            z_hi = z_hi + jnp.where(hit_hi, e, 0.0)
            if it < PEER_TOPK - 1:
                last = PEER_TOPK - it - 1
                for c in range(last):
                    lists[c] = jnp.where(hit_lo, lists[c + 1], lists[c])
                lists[last] = jnp.where(hit_lo, NEG_INF, lists[last])
                heads_hi = jnp.where(hit_hi, NEG_INF, heads_hi)
        z = jnp.sum(z_lo + z_hi, axis=0, keepdims=True)
        n1 = jnp.zeros_like(s1)
        rank2 = jnp.zeros_like(s2)
        for r in range(PEER_TOPK):
            n_r = (n_lo if r < half_tile else n_hi)[r % half_tile:r % half_tile + 1, :]
            n1 = jnp.where(s1 == top1[r], n_r, n1)
            rank2 = rank2 + jnp.where(s2 < top2[r], 1.0, 0.0)
        n1_ref[hd] = n1
        e1_ref[hd] = jnp.exp(s1 - top1[0]) / z
        c2_ref[hd] = rank2.astype(BF16)
        e2_ref[hd] = jnp.exp(s2 - top2[0]).astype(BF16)


def _peer_scores(qp, keys_pad):
    n = qp.shape[0]
    tb = _block(n, 512)
    qw = PEER_HEADS * PEER_KEY_DIM
    big = pl.BlockSpec((PEER_HEADS, N_KEYS, tb), lambda i: (0, 0, i))
    shape_f32 = jax.ShapeDtypeStruct((PEER_HEADS, N_KEYS, n), F32)
    shape_bf16 = jax.ShapeDtypeStruct((PEER_HEADS, N_KEYS, n), BF16)
    return pl.pallas_call(
        _peer_score_kernel,
        grid=(n // tb,),
        in_specs=[
            pl.BlockSpec((tb, qw), lambda i: (i, 0)),
            pl.BlockSpec((PEER_HEADS, 2, N_KEYS, PEER_KEY_DIM), lambda i: (0, 0, 0, 0)),
        ],
        out_specs=[big, big, big, big],
        out_shape=[shape_f32, shape_f32, shape_bf16, shape_bf16],
        compiler_params=_params("parallel"),
        name="peer_scores",
    )(qp, keys_pad)


PEER_KEYS_PER_STEP = V7X_SUBLANES
PEER_EXPERTS_PER_STEP = PEER_KEYS_PER_STEP * N_KEYS


def _peer_gate(n1_ref, e1_ref, c2_ref, e2_ref, g_scr):
    tb = g_scr.shape[1]
    bf16_rows = 2 * V7X_SUBLANES

    def key_row(ref, hd, il):
        row = jnp.broadcast_to(ref[hd, il:il + 1, :], (bf16_rows, tb)).astype(BF16)
        return jnp.tile(row, (N_KEYS // bf16_rows, 1))

    for il in range(PEER_KEYS_PER_STEP):
        gate = jnp.zeros((N_KEYS, tb), BF16)
        for hd in range(PEER_HEADS):
            n_row = key_row(n1_ref, hd, il)
            gate = gate + jnp.where(c2_ref[hd] < n_row, e2_ref[hd], 0.0) * key_row(e1_ref, hd, il)
        g_scr[il * N_KEYS:(il + 1) * N_KEYS, :] = gate


PEER_FIRST_MATMULS = 4
PEER_SECOND_SPLIT = 2


def _peer_mix_kernel(xnt_ref, ed_ref, eu_ref, n1_ref, e1_ref, c2_ref, e2_ref, x1_ref, y_ref, g_scr, w_scr):
    ec = pl.program_id(1)

    @pl.when(ec == 0)
    def _():
        y_ref[...] = x1_ref[...]

    _peer_gate(n1_ref, e1_ref, c2_ref, e2_ref, g_scr)
    tn = (((0,), (0,)), ((), ()))
    n_h = PEER_FIRST_MATMULS
    piece = ed_ref.shape[0] // n_h
    pieces = [slice(c * piece, (c + 1) * piece) for c in range(n_h)]
    hids = [jnp.dot(ed_ref[rows, :], xnt_ref[...], preferred_element_type=F32) for rows in pieces]
    parts = []
    for lo, hi in ((0, PEER_SECOND_SPLIT), (PEER_SECOND_SPLIT, n_h)):
        for q in range(lo, hi):
            w_scr[pieces[q], :] = _gelu(hids[q].astype(BF16)) * g_scr[pieces[q], :]
        rows = slice(lo * piece, hi * piece)
        parts.append(lax.dot_general(w_scr[rows, :], eu_ref[rows, :], tn, preferred_element_type=F32))
    y_ref[...] += parts[0] + parts[1]


def _peer_mix(xnt, e_down, e_up, n1, e1, c2, e2, x1):
    n = xnt.shape[1]
    tb = _block(n, 512)
    ec = PEER_EXPERTS_PER_STEP
    tok = lambda t, e: (t, 0)
    exp_ = lambda t, e: (e, 0)
    key1 = pl.BlockSpec((PEER_HEADS, PEER_KEYS_PER_STEP, tb), lambda t, e: (0, e, t))
    key2 = pl.BlockSpec((PEER_HEADS, N_KEYS, tb), lambda t, e: (0, 0, t))
    return pl.pallas_call(
        _peer_mix_kernel,
        grid=(n // tb, N_EXPERTS // ec),
        in_specs=[
            pl.BlockSpec((D_MODEL, tb), lambda t, e: (0, t)),
            pl.BlockSpec((ec, D_MODEL), exp_),
            pl.BlockSpec((ec, D_MODEL), exp_),
            key1, key1, key2, key2,
            pl.BlockSpec((tb, D_MODEL), tok),
        ],
        out_specs=pl.BlockSpec((tb, D_MODEL), tok),
        out_shape=jax.ShapeDtypeStruct((n, D_MODEL), F32),
        scratch_shapes=[pltpu.VMEM((ec, tb), BF16), pltpu.VMEM((ec, tb), BF16)],
        compiler_params=_params("parallel", "arbitrary"),
        name="peer_mix",
    )(xnt, e_down, e_up, n1, e1, c2, e2, x1)


def _rope_tables(seq):
    t = jnp.arange(seq, dtype=jnp.int32)
    row = (t // GRID_W).astype(F32)
    col = (t % GRID_W).astype(F32)
    half = HEAD_DIM // 2
    inv = ROPE_THETA ** (-jnp.arange(0, half, 2, dtype=F32) / half)
    ar = row[:, None] * inv
    ac = col[:, None] * inv
    cos_t = jnp.concatenate([jnp.cos(ar), jnp.cos(ar), jnp.cos(ac), jnp.cos(ac)], axis=-1)
    sin_t = jnp.concatenate([-jnp.sin(ar), jnp.sin(ar), -jnp.sin(ac), jnp.sin(ac)], axis=-1)
    return cos_t, sin_t


def _gate_weights(w_gate_a, w_gate_x):
    per_tile = V7X_MXU_DIM // LRU_BLOCK_W
    n_tiles = LRU_WIDTH // V7X_MXU_DIM

    def tiles(w):
        w = w.reshape(n_tiles, per_tile, LRU_BLOCK_W, LRU_BLOCK_W)
        eye = jnp.eye(per_tile, dtype=w.dtype)
        return jnp.einsum("tpkj,pq->tpkqj", w, eye).reshape(n_tiles, V7X_MXU_DIM, V7X_MXU_DIM)

    return jnp.concatenate([tiles(w_gate_a), tiles(w_gate_x)], axis=-1).astype(BF16)


def _padded_keys(sub_keys):
    z = jnp.zeros_like(sub_keys[:, 0])
    k0 = jnp.concatenate([sub_keys[:, 0], z], axis=-1)
    k1 = jnp.concatenate([z, sub_keys[:, 1]], axis=-1)
    return jnp.stack([k0, k1], axis=1).astype(BF16)


def _layer(x2, p, batch, seq, cos_t, sin_t):
    q, k, v = _qkv_proj(x2, p["g_mix"], p["w_qkv"], p["g_q"], p["g_k"], cos_t, sin_t, seq)
    attn = _attention(q, k, v, batch, seq)
    xr, yr = _xy_proj(x2, p["g_mix"], p["w_xy"])
    h_fwd, xc = _lru_direction(False, xr, p["conv_w"], p["conv_b"], p["w_gates"][0], p["b_gates"][0],
                               p["lam"][0], batch, seq)
    lru = _lru_direction(True, xc, p["conv_w"], p["conv_b"], p["w_gates"][1], p["b_gates"][1],
                         p["lam"][1], batch, seq, h_fwd=h_fwd, yr=yr)
    x1, xnt, qp = _out_proj(x2, attn, lru, p["g_attn_out"], p["g_lru_out"], p["w_out_a"], p["w_out_l"],
                           p["g_ffn"], p["w_query"])
    n1, e1, c2, e2 = _peer_scores(qp, p["keys_pad"])
    return _peer_mix(xnt, p["e_down"], p["e_up"], n1, e1, c2, e2, x1)


def _trunk(x, layers):
    batch, seq, _ = x.shape
    x2 = x.reshape(batch * seq, D_MODEL)
    cos_t, sin_t = _rope_tables(seq)
    for p in layers:
        x2 = _layer(x2, p, batch, seq, cos_t, sin_t)
    return x2.reshape(batch, seq, D_MODEL)


def _layer_params(l, g_mix, w_in, g_q, g_k, conv_w, conv_b, w_gate_a, b_gate_a, w_gate_x, b_gate_x, lru_lambda,
                  g_attn_out, g_lru_out, w_out, g_ffn, w_query, sub_keys, expert_down, expert_up):
    w_in_b = w_in[l].astype(BF16)
    w_out_b = w_out[l].astype(BF16)
    return {
        "g_mix": g_mix[l][None, :],
        "w_qkv": w_in_b[:, :QKV_WIDTH],
        "w_xy": w_in_b[:, QKV_WIDTH:],
        "g_q": g_q[l][None, :],
        "g_k": g_k[l][None, :],
        "conv_w": conv_w[l],
        "conv_b": conv_b[l][None, :],
        "w_gates": [_gate_weights(w_gate_a[l, d], w_gate_x[l, d]) for d in range(2)],
        "b_gates": [jnp.stack([b_gate_a[l, d], b_gate_x[l, d]], axis=0) for d in range(2)],
        "lam": [lru_lambda[l, d][None, :] for d in range(2)],
        "g_attn_out": g_attn_out[l][None, :],
        "g_lru_out": g_lru_out[l][None, :],
        "w_out_a": w_out_b[:ATTN_WIDTH],
        "w_out_l": w_out_b[ATTN_WIDTH:],
        "g_ffn": g_ffn[l][None, :],
        "w_query": w_query[l].astype(BF16),
        "keys_pad": _padded_keys(sub_keys[l]),
        "e_down": expert_down[l].astype(BF16),
        "e_up": expert_up[l].astype(BF16),
    }


def kernel(x_prompt, x_sample, g_mix, w_in, g_q, g_k, conv_w, conv_b, w_gate_a, b_gate_a, w_gate_x, b_gate_x, lru_lambda, g_attn_out, g_lru_out, w_out, g_ffn, w_query, sub_keys, expert_down, expert_up):
    weights = (g_mix, w_in, g_q, g_k, conv_w, conv_b, w_gate_a, b_gate_a, w_gate_x, b_gate_x, lru_lambda,
               g_attn_out, g_lru_out, w_out, g_ffn, w_query, sub_keys, expert_down, expert_up)
    layers = [_layer_params(l, *weights) for l in range(w_in.shape[0])]
    return (_trunk(x_prompt, layers), _trunk(x_sample, layers))
```
